```python
import math
import jax, jax.numpy as jnp
from jax import lax
import numpy as np

D_MODEL = 1024
BATCH = 16
SEQ = 2048
DEPTH = 2

CTX_LEN = 256
GRID_W = 64
EPS = 1e-6

DIFF_HEADS = 4
DIFF_DK = 64
DIFF_DV = 2 * DIFF_DK
DIFF_WIDTH = DIFF_HEADS * DIFF_DV
DIFF_QK_WIDTH = DIFF_HEADS * 2 * DIFF_DK
ROPE_THETA = 10000.0
Q_BLOCK = 128
FNET_GROUPS = 4
FNET_GROUP_CH = 64
FNET_WIDTH = FNET_GROUPS * FNET_GROUP_CH
S5_CH = 16
S5_GROUPS = 16
S5_STATE = 64
S5_WIDTH = S5_GROUPS * S5_CH
DT_MIN = 1e-3
DT_MAX = 1e-1
D_MIX = DIFF_WIDTH + FNET_WIDTH + S5_WIDTH
IN_WIDTH = 2 * DIFF_QK_WIDTH + DIFF_WIDTH + FNET_WIDTH + S5_WIDTH
F_DENSE = 2816
N_EXPERTS = 8
TOP_K = 2
F_EXPERT = 3584
N_DENSE_LAYERS = (DEPTH + 1) // 2
N_MOE_LAYERS = DEPTH // 2

kernel_name = 'hybrid_diffattn_fnet_s5_moe_prefix_dit'


def rms_norm(x, g):
    xf = x.astype(jnp.float32)
    y = xf * lax.rsqrt(jnp.mean(xf * xf, axis=-1, keepdims=True) + EPS)
    return (y * g.astype(jnp.float32)).astype(x.dtype)


def rope_2d(x, rows, cols):
    half = DIFF_DK // 2
    n_freq = half // 2
    inv = ROPE_THETA ** (-jnp.arange(n_freq, dtype=jnp.float32) / n_freq)
    extra = x.ndim - 3

    def rot(xa, pos):
        ang = pos.astype(jnp.float32)[:, None] * inv
        ang = ang.reshape((ang.shape[0],) + (1,) * extra + (n_freq,))
        cos = jnp.cos(ang).astype(x.dtype)
        sin = jnp.sin(ang).astype(x.dtype)
        x1, x2 = xa[..., :n_freq], xa[..., n_freq:]
        return jnp.concatenate([x1 * cos - x2 * sin, x2 * cos + x1 * sin], axis=-1)

    return jnp.concatenate([rot(x[..., :half], rows), rot(x[..., half:], cols)], axis=-1)


def split_in(p):
    b, l_, _ = p.shape
    q = p[..., :DIFF_QK_WIDTH].reshape(b, l_, DIFF_HEADS, 2, DIFF_DK)
    k = p[..., DIFF_QK_WIDTH:2 * DIFF_QK_WIDTH].reshape(b, l_, DIFF_HEADS, 2, DIFF_DK)
    o = 2 * DIFF_QK_WIDTH
    v = p[..., o:o + DIFF_WIDTH].reshape(b, l_, DIFF_HEADS, DIFF_DV)
    o = o + DIFF_WIDTH
    f = p[..., o:o + FNET_WIDTH]
    u = p[..., o + FNET_WIDTH:]
    return q, k, v, f, u


def diff_attend(q1, q2, k1, k2, v, lam):
    scale = DIFF_DK ** -0.5
    p1 = jax.nn.softmax(jnp.einsum('bhqd,bhkd->bhqk', q1, k1).astype(jnp.float32) * scale, axis=-1)
    p2 = jax.nn.softmax(jnp.einsum('bhqd,bhkd->bhqk', q2, k2).astype(jnp.float32) * scale, axis=-1)
    return jnp.einsum('bhqk,bhkd->bhqd', (p1 - lam * p2).astype(v.dtype), v)


def diff_attention(q, k, v, qc, kc, vc, lam, lam_init, subln, rows, cols, need_ctx):
    q = rope_2d(q, rows, cols)
    k = rope_2d(k, rows, cols)
    heads = lambda t: jnp.moveaxis(t, 2, 1)
    q, k, v, qc, kc, vc = (heads(t) for t in (q, k, v, qc, kc, vc))
    k_all = jnp.concatenate([kc, k], axis=2)
    v_all = jnp.concatenate([vc, v], axis=2)
    k1, k2 = k_all[..., 0, :], k_all[..., 1, :]
    b, h, s = q.shape[:3]
    nblk = s // Q_BLOCK
    qb = jnp.moveaxis(q.reshape(b, h, nblk, Q_BLOCK, 2, DIFF_DK), 2, 0)
    o = lax.map(lambda qq: diff_attend(qq[..., 0, :], qq[..., 1, :], k1, k2, v_all, lam), qb)
    o = jnp.moveaxis(o, 0, 2).reshape(b, h, s, DIFF_DV)

    def finish(t):
        t = rms_norm(t, subln) * (1.0 - lam_init)
        return jnp.moveaxis(t, 1, 2).reshape(t.shape[0], t.shape[2], DIFF_WIDTH)

    out_c = None
    if need_ctx:
        out_c = finish(diff_attend(qc[..., 0, :], qc[..., 1, :], kc[..., 0, :], kc[..., 1, :], vc, lam))
    return finish(o), out_c


def fourier_mix(f, w):
    b, l_, _ = f.shape
    g = f.astype(jnp.float32).reshape(b, l_, FNET_GROUPS, FNET_GROUP_CH)
    mixed = jnp.fft.fft2(g, axes=(1, 3), norm='ortho').real.reshape(b, l_, FNET_WIDTH).astype(f.dtype)
    return mixed @ w


def s5_discretize(a_re, a_im, log_dt, b_re, b_im):
    f32 = jnp.float32
    a_re, a_im = a_re.astype(f32), a_im.astype(f32)
    b_re, b_im = b_re.astype(f32), b_im.astype(f32)
    dt = jnp.exp(log_dt.astype(f32))[:, None]
    mag = jnp.exp(a_re * dt)
    lr, li = mag * jnp.cos(a_im * dt), mag * jnp.sin(a_im * dt)
    nr, ni = lr - 1.0, li
    den = a_re * a_re + a_im * a_im
    cr = (nr * a_re + ni * a_im) / den
    ci = (ni * a_re - nr * a_im) / den
    bbr = cr[..., None] * b_re - ci[..., None] * b_im
    bbi = cr[..., None] * b_im + ci[..., None] * b_re
    return lr, li, bbr, bbi


def complex_scan(lr, li, br, bi, h0r, h0i, reverse):
    l_ = br.shape[1]
    if h0r is not None:
        first = -1 if reverse else 0
        br = br.at[:, first].add(lr * h0r - li * h0i)
        bi = bi.at[:, first].add(lr * h0i + li * h0r)
    ar = jnp.broadcast_to(lr, (1, l_) + lr.shape)
    ai = jnp.broadcast_to(li, (1, l_) + li.shape)

    def combine(e1, e2):
        a1r, a1i, b1r, b1i = e1
        a2r, a2i, b2r, b2i = e2
        return (a2r * a1r - a2i * a1i, a2r * a1i + a2i * a1r,
                a2r * b1r - a2i * b1i + b2r, a2r * b1i + a2i * b1r + b2i)

    _, _, hr, hi = lax.associative_scan(combine, (ar, ai, br, bi), reverse=reverse, axis=1)
    return hr, hi


def s5_readout(hr, hi, c_re, c_im):
    f32 = jnp.float32
    return jnp.einsum('blgp,gcp->blgc', hr, c_re.astype(f32)) - jnp.einsum('blgp,gcp->blgc', hi, c_im.astype(f32))


def s5_mixer(u, uc, a_re, a_im, log_dt, b_re, b_im, c_re, c_im, d, w_glu, need_ctx):
    f32 = jnp.float32

    def drive(t, bbr, bbi):
        tg = t.astype(f32).reshape(t.shape[0], t.shape[1], S5_GROUPS, S5_CH)
        return jnp.einsum('blgc,gpc->blgp', tg, bbr), jnp.einsum('blgc,gpc->blgp', tg, bbi)

    y, yc = 0.0, 0.0
    for direction, reverse in ((0, False), (1, True)):
        lr, li, bbr, bbi = s5_discretize(a_re[direction], a_im[direction], log_dt[direction],
                                         b_re[direction], b_im[direction])
        hcr, hci = complex_scan(lr, li, *drive(uc, bbr, bbi), None, None, reverse)
        end = 0 if reverse else -1
        hr, hi = complex_scan(lr, li, *drive(u, bbr, bbi), hcr[:, end], hci[:, end], reverse)
        y = y + s5_readout(hr, hi, c_re[direction], c_im[direction])
        if need_ctx:
            yc = yc + s5_readout(hcr, hci, c_re[direction], c_im[direction])

    def finish(yy, t):
        yy = yy.reshape(t.shape[0], t.shape[1], S5_WIDTH) + d.astype(f32) * t.astype(f32)
        yy = jax.nn.gelu(yy)
        return (yy * jax.nn.sigmoid(yy @ w_glu.astype(f32))).astype(t.dtype)

    return finish(y, u), (finish(yc, uc) if need_ctx else None)


def swiglu(t, wg, wu, wd):
    return (jax.nn.silu(t @ wg) * (t @ wu)) @ wd


def moe_swiglu(t, router, wg, wu, wd):
    b, l_, d_ = t.shape
    tok = t.reshape(-1, d_)
    probs = jax.nn.softmax((tok @ router).astype(jnp.float32), axis=-1)
    top_p, top_i = lax.top_k(probs, TOP_K)
    top_p = top_p / jnp.sum(top_p, axis=-1, keepdims=True)
    gates = jnp.sum(jax.nn.one_hot(top_i, N_EXPERTS, dtype=jnp.float32) * top_p[..., None], axis=1)
    out = jnp.zeros_like(tok)
    for e in range(N_EXPERTS):
        out = out + gates[:, e:e + 1].astype(t.dtype) * swiglu(tok, wg[e], wu[e], wd[e])
    return out.reshape(b, l_, d_)


def setup_inputs(seed: int = 0) -> dict:
    key = jax.random.key(seed)
    ks = iter(jax.random.split(key, 48))
    f32 = jnp.float32
    nrm = lambda shape, scale: jax.random.normal(next(ks), shape, f32) * scale
    G, P, C, D = S5_GROUPS, S5_STATE, S5_CH, D_MODEL
    a_im_init = jnp.pi * jnp.arange(P, dtype=f32)
    return {
        'x': nrm((BATCH, SEQ, D), 1.0),
        'c': nrm((BATCH, D), 1.0),
        'ctx': nrm((BATCH, CTX_LEN, D), 1.0),
        'c_ctx': nrm((D,), 1.0),
        'ada_w': nrm((DEPTH, D, 6 * D), 0.5 * D ** -0.5),
        'ada_b': nrm((DEPTH, 6 * D), 0.02),
        'norm_mix_pre': 1.0 + nrm((DEPTH, D), 0.05),
        'norm_mix_post': 1.0 + nrm((DEPTH, D), 0.05),
        'norm_ffn_pre': 1.0 + nrm((DEPTH, D), 0.05),
        'norm_ffn_post': 1.0 + nrm((DEPTH, D), 0.05),
        'w_in': nrm((DEPTH, D, IN_WIDTH), D ** -0.5),
        'w_out': nrm((DEPTH, D_MIX, D), D_MIX ** -0.5),
        'diff_lq1': nrm((DEPTH, DIFF_DK), 0.1),
        'diff_lk1': nrm((DEPTH, DIFF_DK), 0.1),
        'diff_lq2': nrm((DEPTH, DIFF_DK), 0.1),
        'diff_lk2': nrm((DEPTH, DIFF_DK), 0.1),
        'diff_subln': 1.0 + nrm((DEPTH, DIFF_DV), 0.05),
        'fnet_w': nrm((DEPTH, FNET_WIDTH, FNET_WIDTH), FNET_WIDTH ** -0.5),
        's5_a_re': -0.5 + nrm((DEPTH, 2, G, P), 0.01),
        's5_a_im': a_im_init + nrm((DEPTH, 2, G, P), 0.01),
        's5_log_dt': jax.random.uniform(next(ks), (DEPTH, 2, G), f32, math.log(DT_MIN), math.log(DT_MAX)),
        's5_b_re': nrm((DEPTH, 2, G, P, C), (2 * C) ** -0.5),
        's5_b_im': nrm((DEPTH, 2, G, P, C), (2 * C) ** -0.5),
        's5_c_re': nrm((DEPTH, 2, G, C, P), (2 * P) ** -0.5),
        's5_c_im': nrm((DEPTH, 2, G, C, P), (2 * P) ** -0.5),
        's5_d': nrm((DEPTH, S5_WIDTH), 1.0),
        's5_w_glu': nrm((DEPTH, S5_WIDTH, S5_WIDTH), S5_WIDTH ** -0.5),
        'ffn_w_gate': nrm((N_DENSE_LAYERS, D, F_DENSE), D ** -0.5),
        'ffn_w_up': nrm((N_DENSE_LAYERS, D, F_DENSE), D ** -0.5),
        'ffn_w_down': nrm((N_DENSE_LAYERS, F_DENSE, D), F_DENSE ** -0.5),
        'moe_router': nrm((N_MOE_LAYERS, D, N_EXPERTS), D ** -0.5),
        'moe_w_gate': nrm((N_MOE_LAYERS, N_EXPERTS, D, F_EXPERT), D ** -0.5),
        'moe_w_up': nrm((N_MOE_LAYERS, N_EXPERTS, D, F_EXPERT), D ** -0.5),
        'moe_w_down': nrm((N_MOE_LAYERS, N_EXPERTS, F_EXPERT, D), F_EXPERT ** -0.5),
    }


def reference(x, c, ctx, c_ctx, ada_w, ada_b, norm_mix_pre, norm_mix_post, norm_ffn_pre, norm_ffn_post,
              w_in, w_out, diff_lq1, diff_lk1, diff_lq2, diff_lk2, diff_subln, fnet_w,
              s5_a_re, s5_a_im, s5_log_dt, s5_b_re, s5_b_im, s5_c_re, s5_c_im, s5_d, s5_w_glu,
              ffn_w_gate, ffn_w_up, ffn_w_down, moe_router, moe_w_gate, moe_w_up, moe_w_down):
    f32 = jnp.float32
    s = x.shape[1]
    ROWS = s // GRID_W
    rows = jnp.repeat(jnp.arange(ROWS, dtype=jnp.int32), GRID_W)
    cols = jnp.tile(jnp.arange(GRID_W, dtype=jnp.int32), ROWS)
    silu_c = jax.nn.silu(c)
    silu_cc = jax.nn.silu(c_ctx)
    xc = ctx
    for l in range(DEPTH):
        need_ctx = l < DEPTH - 1
        mod = silu_c @ ada_w[l] + ada_b[l]
        mod_c = silu_cc @ ada_w[l] + ada_b[l]
        sh_m, sc_m, g_m, sh_f, sc_f, g_f = jnp.split(mod[:, None, :], 6, axis=-1)
        csh_m, csc_m, cg_m, csh_f, csc_f, cg_f = jnp.split(mod_c, 6, axis=-1)

        h = rms_norm(x, norm_mix_pre[l]) * (1.0 + sc_m) + sh_m
        hc = rms_norm(xc, norm_mix_pre[l]) * (1.0 + csc_m) + csh_m
        q, k, v, f, u = split_in(h @ w_in[l])
        qc, kc, vc, fc, uc = split_in(hc @ w_in[l])

        lam_init = 0.8 - 0.6 * math.exp(-0.3 * l)
        lam = (jnp.exp(jnp.sum(diff_lq1[l].astype(f32) * diff_lk1[l].astype(f32)))
               - jnp.exp(jnp.sum(diff_lq2[l].astype(f32) * diff_lk2[l].astype(f32))) + lam_init)
        a_lat, a_ctx = diff_attention(q, k, v, qc, kc, vc, lam, lam_init, diff_subln[l], rows, cols, need_ctx)
        f_lat = fourier_mix(f, fnet_w[l])
        s_lat, s_ctx = s5_mixer(u, uc, s5_a_re[l], s5_a_im[l], s5_log_dt[l], s5_b_re[l], s5_b_im[l],
                                s5_c_re[l], s5_c_im[l], s5_d[l], s5_w_glu[l], need_ctx)
        mix = jnp.concatenate([a_lat, f_lat, s_lat], axis=-1) @ w_out[l]
        x = x + g_m * rms_norm(mix, norm_mix_post[l])
        if need_ctx:
            mix_c = jnp.concatenate([a_ctx, fourier_mix(fc, fnet_w[l]), s_ctx], axis=-1) @ w_out[l]
            xc = xc + cg_m * rms_norm(mix_c, norm_mix_post[l])

        i = l // 2
        if l % 2 == 0:
            ffn = lambda t: swiglu(t, ffn_w_gate[i], ffn_w_up[i], ffn_w_down[i])
        else:
            ffn = lambda t: moe_swiglu(t, moe_router[i], moe_w_gate[i], moe_w_up[i], moe_w_down[i])
        h = rms_norm(x, norm_ffn_pre[l]) * (1.0 + sc_f) + sh_f
        x = x + g_f * rms_norm(ffn(h), norm_ffn_post[l])
        if need_ctx:
            hc = rms_norm(xc, norm_ffn_pre[l]) * (1.0 + csc_f) + csh_f
            xc = xc + cg_f * rms_norm(ffn(hc), norm_ffn_post[l])
    return x
```

```python
import functools
import math

import jax
import jax.numpy as jnp
from jax import lax
from jax.experimental import pallas as pl
from jax.experimental.pallas import tpu as pltpu

F32 = jnp.float32
BF16 = jnp.bfloat16

D_MODEL = 1024
DEPTH = 2
GRID_W = 64
EPS = 1e-6
DIFF_HEADS = 4
DIFF_DK = 64
DIFF_DV = 2 * DIFF_DK
DIFF_WIDTH = DIFF_HEADS * DIFF_DV
DIFF_QK_WIDTH = DIFF_HEADS * 2 * DIFF_DK
ROPE_THETA = 10000.0
FNET_GROUPS = 4
FNET_GROUP_CH = 64
FNET_WIDTH = FNET_GROUPS * FNET_GROUP_CH
S5_CH = 16
S5_GROUPS = 16
S5_STATE = 64
S5_WIDTH = S5_GROUPS * S5_CH
S5_LANES = S5_GROUPS * S5_STATE
IN_WIDTH = 2 * DIFF_QK_WIDTH + DIFF_WIDTH + FNET_WIDTH + S5_WIDTH
N_EXPERTS = 8
LANES = 128
MOD_ROWS = 24

VMEM_LIMIT = 56 * 1024 * 1024


def _params(sem):
    return pltpu.CompilerParams(dimension_semantics=sem, vmem_limit_bytes=VMEM_LIMIT)


def _rms(x, g):
    return x * lax.rsqrt(jnp.mean(x * x, axis=-1, keepdims=True) + EPS) * g


def _dot(a, b):
    return jnp.dot(a, b, preferred_element_type=F32)


def _mod_kernel(c_ref, w_ref, b_ref, o_ref):
    c = c_ref[...]
    sc = c * jax.nn.sigmoid(c)
    o_ref[0] = jnp.dot(sc, w_ref[0], preferred_element_type=F32,
                       precision=lax.Precision.HIGHEST) + b_ref[0]


def _modulation(cc, ada_w, ada_b):
    depth, d, n = ada_w.shape
    tn = 1536
    return pl.pallas_call(
        _mod_kernel,
        out_shape=jax.ShapeDtypeStruct((depth, MOD_ROWS, n), F32),
        grid=(depth, n // tn),
        in_specs=[pl.BlockSpec((MOD_ROWS, d), lambda l, j: (0, 0)),
                  pl.BlockSpec((1, d, tn), lambda l, j: (l, 0, j)),
                  pl.BlockSpec((1, 1, tn), lambda l, j: (l, 0, j))],
        out_specs=pl.BlockSpec((1, MOD_ROWS, tn), lambda l, j: (l, 0, j)),
        compiler_params=_params(("parallel", "parallel")),
        name="adaln_mod",
    )(cc, ada_w, ada_b.reshape(depth, 1, n))


def _inproj_kernel(*refs, rope):
    if rope:
        (x_ref, sc_ref, sh_ref, g_ref, w_ref, cs_ref, cos_ref, sina_ref, sinb_ref,
         q_ref, k_ref, v_ref, g1_ref, g2_ref, u_ref) = refs
    else:
        (x_ref, sc_ref, sh_ref, g_ref, w_ref, cs_ref,
         q_ref, k_ref, v_ref, g1_ref, g2_ref, u_ref) = refs
    h = _rms(x_ref[0], g_ref[...]) * (1.0 + sc_ref[0]) + sh_ref[0]
    hb = h.astype(BF16)

    def proj(lo, hi):
        return _dot(hb, w_ref[:, lo:hi])

    def rotate(t):
        outs = []
        for j in range(t.shape[1] // LANES):
            tb = t[:, j * LANES:(j + 1) * LANES]
            outs.append(tb * cos_ref[...]
                        + pltpu.roll(tb, LANES - 16, axis=1) * sina_ref[...]
                        + pltpu.roll(tb, 16, axis=1) * sinb_ref[...])
        return jnp.concatenate(outs, axis=1)

    q = proj(0, DIFF_QK_WIDTH)
    k = proj(DIFF_QK_WIDTH, 2 * DIFF_QK_WIDTH)
    if rope:
        q = rotate(q)
        k = rotate(k)
    q_ref[0] = (q * (DIFF_DK ** -0.5)).astype(BF16)
    k_ref[0] = k.astype(BF16)
    o = 2 * DIFF_QK_WIDTH
    v_ref[0] = proj(o, o + DIFF_WIDTH).astype(BF16)
    o += DIFF_WIDTH
    f = proj(o, o + FNET_WIDTH).astype(BF16)
    g12 = _dot(f, cs_ref[...])
    g1_ref[...] = g12[:, :FNET_WIDTH].astype(BF16)
    g2_ref[...] = g12[:, FNET_WIDTH:].astype(BF16)
    u_ref[...] = proj(o + FNET_WIDTH, IN_WIDTH)


def _inproj(x, sc, sh, g, w_bf, cs64, rope_tabs, tl):
    b, l, d = x.shape
    rope = rope_tabs is not None
    in_specs = [pl.BlockSpec((1, tl, d), lambda i, bb: (bb, i, 0)),
                pl.BlockSpec((1, 1, d), lambda i, bb: (bb, 0, 0)),
                pl.BlockSpec((1, 1, d), lambda i, bb: (bb, 0, 0)),
                pl.BlockSpec((1, d), lambda i, bb: (0, 0)),
                pl.BlockSpec((d, IN_WIDTH), lambda i, bb: (0, 0)),
                pl.BlockSpec((FNET_WIDTH, 2 * FNET_WIDTH), lambda i, bb: (0, 0))]
    args = [x, sc, sh, g, w_bf, cs64]
    if rope:
        in_specs += [pl.BlockSpec((tl, LANES), lambda i, bb: (i, 0))] * 3
        args += list(rope_tabs)
    tok = lambda w: pl.BlockSpec((1, tl, w), lambda i, bb: (bb, i, 0))
    tb = pl.BlockSpec((tl, FNET_WIDTH), lambda i, bb: (i, bb))
    return pl.pallas_call(
        functools.partial(_inproj_kernel, rope=rope),
        out_shape=(jax.ShapeDtypeStruct((b, l, DIFF_QK_WIDTH), BF16),
                   jax.ShapeDtypeStruct((b, l, DIFF_QK_WIDTH), BF16),
                   jax.ShapeDtypeStruct((b, l, DIFF_WIDTH), BF16),
                   jax.ShapeDtypeStruct((l, b * FNET_WIDTH), BF16),
                   jax.ShapeDtypeStruct((l, b * FNET_WIDTH), BF16),
                   jax.ShapeDtypeStruct((l, b * S5_WIDTH), F32)),
        grid=(l // tl, b),
        in_specs=in_specs,
        out_specs=(tok(DIFF_QK_WIDTH), tok(DIFF_QK_WIDTH), tok(DIFF_WIDTH), tb, tb, tb),
        compiler_params=_params(("parallel", "parallel")),
        name="inproj_rope" if rope else "inproj",
    )(*args)


def _attn_kernel(*refs, n_src, lam_init):
    q_ref = refs[0]
    kv = refs[1:1 + 2 * n_src]
    lq1, lk1, lq2, lk2, sub_ref, o_ref, k1_s, k2_s, v_s = refs[1 + 2 * n_src:]

    @pl.when(pl.program_id(2) == 0)
    def _():
        off = 0
        for s in range(n_src):
            kk = kv[2 * s][0]
            n = kk.shape[0]
            k1_s[off:off + n, :] = kk[:, :DIFF_DK]
            k2_s[off:off + n, :] = kk[:, DIFF_DK:]
            v_s[off:off + n, :] = kv[2 * s + 1][0]
            off += n

    lam = (jnp.exp(jnp.sum(lq1[...] * lk1[...], axis=-1, keepdims=True))
           - jnp.exp(jnp.sum(lq2[...] * lk2[...], axis=-1, keepdims=True)) + lam_init)
    q = q_ref[0]

    def softmax_parts(qj, k_s):
        s = lax.dot_general(qj, k_s[...], (((1,), (1,)), ((), ())), preferred_element_type=F32)
        p = jnp.exp(s - jnp.max(s, axis=-1, keepdims=True))
        return p, jnp.sum(p, axis=-1, keepdims=True)

    p1, l1 = softmax_parts(q[:, :DIFF_DK], k1_s)
    p2, l2 = softmax_parts(q[:, DIFF_DK:], k2_s)
    p = p1 * (1.0 / l1) - p2 * (lam / l2)
    o = _dot(p.astype(BF16), v_s[...])
    o_ref[0] = (_rms(o, sub_ref[...]) * (1.0 - lam_init)).astype(BF16)


def _attention(q, kv_srcs, lams, subln, lam_init, tq):
    b, lq, _ = q.shape
    n_src = len(kv_srcs)
    lk = sum(k.shape[1] for k, _ in kv_srcs)
    in_specs = [pl.BlockSpec((1, tq, DIFF_DV), lambda bb, h, i: (bb, i, h))]
    args = [q]
    for k, v in kv_srcs:
        spec = pl.BlockSpec((1, k.shape[1], DIFF_DV), lambda bb, h, i: (bb, 0, h))
        in_specs += [spec, spec]
        args += [k, v]
    in_specs += [pl.BlockSpec((1, DIFF_DK), lambda bb, h, i: (0, 0))] * 4
    in_specs += [pl.BlockSpec((1, DIFF_DV), lambda bb, h, i: (0, 0))]
    args += list(lams) + [subln]
    return pl.pallas_call(
        functools.partial(_attn_kernel, n_src=n_src, lam_init=lam_init),
        out_shape=jax.ShapeDtypeStruct((b, lq, DIFF_WIDTH), BF16),
        grid=(b, DIFF_HEADS, lq // tq),
        in_specs=in_specs,
        out_specs=pl.BlockSpec((1, tq, DIFF_DV), lambda bb, h, i: (bb, i, h)),
        scratch_shapes=[pltpu.VMEM((lk, DIFF_DK), BF16), pltpu.VMEM((lk, DIFF_DK), BF16),
                        pltpu.VMEM((lk, DIFF_DV), BF16)],
        compiler_params=_params(("parallel", "parallel", "arbitrary")),
        name="diff_attn_%d" % n_src,
    )(*args)


def _dft_kernel(c_ref, s_ref, g1_ref, g2_ref, w_ref, o_ref):
    z = _dot(c_ref[...], g1_ref[...]) - _dot(s_ref[...], g2_ref[...])
    for j in range(o_ref.shape[0]):
        zj = z[:, j * FNET_WIDTH:(j + 1) * FNET_WIDTH].astype(BF16)
        o_ref[j] = _dot(zj, w_ref[...]).astype(BF16)


def _fnet_dft(cosm, sinm, g1, g2, w_bf, b, tm, nb):
    l = cosm.shape[0]
    tn = nb * FNET_WIDTH
    return pl.pallas_call(
        _dft_kernel,
        out_shape=jax.ShapeDtypeStruct((b, l, FNET_WIDTH), BF16),
        grid=(l // tm, b // nb),
        in_specs=[pl.BlockSpec((tm, l), lambda i, j: (i, 0)),
                  pl.BlockSpec((tm, l), lambda i, j: (i, 0)),
                  pl.BlockSpec((l, tn), lambda i, j: (0, j)),
                  pl.BlockSpec((l, tn), lambda i, j: (0, j)),
                  pl.BlockSpec((FNET_WIDTH, FNET_WIDTH), lambda i, j: (0, 0))],
        out_specs=pl.BlockSpec((nb, tm, FNET_WIDTH), lambda i, j: (j, i, 0)),
        compiler_params=_params(("parallel", "parallel")),
        name="fnet_dft",
    )(cosm, sinm, g1, g2, w_bf)


def _s5_kernel(uf_ref, ur_ref, bf_ref, br_ref, lf_ref, lr_ref, cf_ref, cr_ref, h0f_ref, h0r_ref,
               yf_ref, yr_ref, hef_ref, her_ref, hs_f, hs_r, drv, hbuf, *, tc, nb):
    j = pl.program_id(0)

    @pl.when(j == 0)
    def _():
        hs_f[...] = h0f_ref[...]
        hs_r[...] = h0r_ref[...]

    def run(u_ref, b_ref, l_ref, c_ref, hs, y_ref, reverse):
        drv[...] = _dot(u_ref[...].astype(BF16), b_ref[...])

        def step(t, carry):
            hr, hi = carry
            lre = l_ref[:, :S5_LANES]
            lim = l_ref[:, S5_LANES:]
            tt = tc - 1 - t if reverse else t
            rows = pl.ds(pl.multiple_of(tt * nb, nb), nb)
            d = drv[rows, :]
            nr = lre * hr - lim * hi + d[:, :S5_LANES]
            ni = lre * hi + lim * hr + d[:, S5_LANES:]
            hbuf[rows, :S5_LANES] = nr.astype(BF16)
            hbuf[rows, S5_LANES:] = ni.astype(BF16)
            return nr, ni

        hr, hi = lax.fori_loop(0, tc, step, (hs[:, :S5_LANES], hs[:, S5_LANES:]))
        hs[:, :S5_LANES] = hr
        hs[:, S5_LANES:] = hi
        y_ref[...] = _dot(hbuf[...], c_ref[...])

    run(uf_ref, bf_ref, lf_ref, cf_ref, hs_f, yf_ref, False)
    run(ur_ref, br_ref, lr_ref, cr_ref, hs_r, yr_ref, True)

    @pl.when(j == pl.num_programs(0) - 1)
    def _():
        hef_ref[...] = hs_f[...]
        her_ref[...] = hs_r[...]


def _s5_scan(u_tb, mats, h0f, h0r, nb, tc):
    rows = u_tb.shape[0]
    n = rows // (tc * nb)
    r = tc * nb
    bmf, bmr, lf, lr, cmf, cmr = mats
    full = lambda a: pl.BlockSpec(a.shape, lambda j: (0,) * a.ndim)
    fwd = pl.BlockSpec((r, S5_WIDTH), lambda j: (j, 0))
    rev = pl.BlockSpec((r, S5_WIDTH), lambda j: (n - 1 - j, 0))
    st = jax.ShapeDtypeStruct((nb, 2 * S5_LANES), F32)
    return pl.pallas_call(
        functools.partial(_s5_kernel, tc=tc, nb=nb),
        out_shape=(jax.ShapeDtypeStruct((rows, S5_WIDTH), F32),
                   jax.ShapeDtypeStruct((rows, S5_WIDTH), F32), st, st),
        grid=(n,),
        in_specs=[fwd, rev, full(bmf), full(bmr), full(lf), full(lr), full(cmf), full(cmr),
                  full(h0f), full(h0r)],
        out_specs=(fwd, rev, full(h0f), full(h0r)),
        scratch_shapes=[pltpu.VMEM((nb, 2 * S5_LANES), F32), pltpu.VMEM((nb, 2 * S5_LANES), F32),
                        pltpu.VMEM((r, 2 * S5_LANES), F32), pltpu.VMEM((r, 2 * S5_LANES), BF16)],
        compiler_params=_params(("arbitrary",)),
        name="s5_scan",
    )(u_tb, u_tb, bmf, bmr, lf, lr, cmf, cmr, h0f, h0r)


def _s5_matrices(a_re, a_im, log_dt, b_re, b_im, c_re, c_im):
    dt = jnp.exp(log_dt)[:, None]
    mag = jnp.exp(a_re * dt)
    lr, li = mag * jnp.cos(a_im * dt), mag * jnp.sin(a_im * dt)
    nr, ni = lr - 1.0, li
    den = a_re * a_re + a_im * a_im
    cr = (nr * a_re + ni * a_im) / den
    ci = (ni * a_re - nr * a_im) / den
    bbr = cr[..., None] * b_re - ci[..., None] * b_im
    bbi = cr[..., None] * b_im + ci[..., None] * b_re
    eye = jnp.eye(S5_GROUPS, dtype=F32)

    def drive_mat(bb):
        return jnp.einsum('gpc,gh->gchp', bb, eye).reshape(S5_WIDTH, S5_LANES)

    def read_mat(cc):
        return jnp.einsum('gcp,gh->gphc', cc, eye).reshape(S5_LANES, S5_WIDTH)

    bm = jnp.concatenate([drive_mat(bbr), drive_mat(bbi)], axis=1).astype(BF16)
    cm = jnp.concatenate([read_mat(c_re), -read_mat(c_im)], axis=0).astype(BF16)
    lam = jnp.concatenate([lr.reshape(1, S5_LANES), li.reshape(1, S5_LANES)], axis=1)
    return bm, lam, cm


def _outproj_kernel(a_ref, f_ref, yf_ref, yr_ref, u_ref, d_ref, wglu_ref, wa_ref, wf_ref, ws_ref,
                    x_ref, gate_ref, g_ref, o_ref):
    yy = yf_ref[...] + yr_ref[...] + d_ref[...] * u_ref[...]
    yy = jax.nn.gelu(yy)
    s = yy * jax.nn.sigmoid(_dot(yy.astype(BF16), wglu_ref[...]))
    mix = (_dot(a_ref[0], wa_ref[...]) + _dot(f_ref[0], wf_ref[...])
           + _dot(s.astype(BF16), ws_ref[...]))
    o_ref[0] = x_ref[0] + gate_ref[0] * _rms(mix, g_ref[...])


def _outproj(a, fo, yf, yr, u, d, wglu, w_out, x, gate, g, tl):
    b, l, dm = x.shape
    wa, wf, ws = (w_out[:DIFF_WIDTH], w_out[DIFF_WIDTH:DIFF_WIDTH + FNET_WIDTH],
                  w_out[DIFF_WIDTH + FNET_WIDTH:])
    tok = lambda w: pl.BlockSpec((1, tl, w), lambda i, bb: (bb, i, 0))
    tb = pl.BlockSpec((tl, S5_WIDTH), lambda i, bb: (i, bb))
    full = lambda arr: pl.BlockSpec(arr.shape, lambda i, bb: (0,) * arr.ndim)
    return pl.pallas_call(
        _outproj_kernel,
        out_shape=jax.ShapeDtypeStruct((b, l, dm), F32),
        grid=(l // tl, b),
        in_specs=[tok(DIFF_WIDTH), tok(FNET_WIDTH), tb, tb, tb, full(d), full(wglu),
                  full(wa), full(wf), full(ws), tok(dm),
                  pl.BlockSpec((1, 1, dm), lambda i, bb: (bb, 0, 0)), full(g)],
        out_specs=tok(dm),
        compiler_params=_params(("parallel", "parallel")),
        name="outproj",
    )(a, fo, yf, yr, u, d, wglu, wa, wf, ws, x, gate, g)


def _route(h, r_ref):
    hi = h.astype(BF16)
    lo = (h - hi.astype(F32)).astype(BF16)
    logits = _dot(hi, r_ref[0]) + _dot(hi, r_ref[1]) + _dot(lo, r_ref[0])
    lane = lax.broadcasted_iota(jnp.int32, logits.shape, 1).astype(F32)
    neg = jnp.float32(-jnp.inf)
    logits = jnp.where(lane < N_EXPERTS, logits, neg)
    m1 = jnp.max(logits, axis=-1, keepdims=True)
    i1 = jnp.min(jnp.where(logits == m1, lane, float(LANES)), axis=-1, keepdims=True)
    rest = jnp.where(lane == i1, neg, logits)
    m2 = jnp.max(rest, axis=-1, keepdims=True)
    i2 = jnp.min(jnp.where(rest == m2, lane, float(LANES)), axis=-1, keepdims=True)
    e2 = jnp.exp(m2 - m1)
    w1 = 1.0 / (1.0 + e2)
    return jnp.where(lane == i1, w1, 0.0) + jnp.where(lane == i2, e2 * w1, 0.0)


def _ffn_kernel(*refs, routed, fsub):
    if routed:
        (x_ref, sc_ref, sh_ref, g_ref, r_ref, wg_ref, wu_ref, wd_ref, gate_ref, gp_ref,
         o_ref, h_s, acc_s, gates_s) = refs
    else:
        (x_ref, sc_ref, sh_ref, g_ref, wg_ref, wu_ref, wd_ref, gate_ref, gp_ref,
         o_ref, h_s, acc_s) = refs
    e = pl.program_id(2)
    f = pl.program_id(3)

    @pl.when((e == 0) & (f == 0))
    def _():
        h = _rms(x_ref[0], g_ref[...]) * (1.0 + sc_ref[0]) + sh_ref[0]
        h_s[...] = h.astype(BF16)
        acc_s[...] = jnp.zeros_like(acc_s)
        if routed:
            gates_s[...] = _route(h, r_ref)

    hb = h_s[...]
    tf = wg_ref.shape[2]
    part = None
    for c in range(tf // fsub):
        sl = slice(c * fsub, (c + 1) * fsub)
        gt = _dot(hb, wg_ref[0, :, sl])
        up = _dot(hb, wu_ref[0, :, sl])
        act = (gt * jax.nn.sigmoid(gt) * up).astype(BF16)
        contrib = _dot(act, wd_ref[0, sl, :])
        part = contrib if part is None else part + contrib
    if routed:
        lane = lax.broadcasted_iota(jnp.int32, gates_s.shape, 1)
        ge = jnp.sum(jnp.where(lane == e, gates_s[...], 0.0), axis=-1, keepdims=True)
        part = ge * part
    acc_s[...] += part

    @pl.when((e == pl.num_programs(2) - 1) & (f == pl.num_programs(3) - 1))
    def _():
        o_ref[0] = x_ref[0] + gate_ref[0] * _rms(acc_s[...], gp_ref[...])


def _ffn(x, sc, sh, g, router, wg, wu, wd, gate, gp, tm, tf, fsub):
    b, l, d = x.shape
    ne, _, fdim = wg.shape
    routed = router is not None
    tok = pl.BlockSpec((1, tm, d), lambda bb, i, e, f: (bb, i, 0))
    modv = pl.BlockSpec((1, 1, d), lambda bb, i, e, f: (bb, 0, 0))
    vec = pl.BlockSpec((1, d), lambda bb, i, e, f: (0, 0))
    in_specs = [tok, modv, modv, vec]
    args = [x, sc, sh, g]
    if routed:
        in_specs.append(pl.BlockSpec(router.shape, lambda bb, i, e, f: (0, 0, 0)))
        args.append(router)
    in_specs += [pl.BlockSpec((1, d, tf), lambda bb, i, e, f: (e, 0, f)),
                 pl.BlockSpec((1, d, tf), lambda bb, i, e, f: (e, 0, f)),
                 pl.BlockSpec((1, tf, d), lambda bb, i, e, f: (e, f, 0)),
                 modv, vec]
    args += [wg, wu, wd, gate, gp]
    scratch = [pltpu.VMEM((tm, d), BF16), pltpu.VMEM((tm, d), F32)]
    if routed:
        scratch.append(pltpu.VMEM((tm, LANES), F32))
    return pl.pallas_call(
        functools.partial(_ffn_kernel, routed=routed, fsub=fsub),
        out_shape=jax.ShapeDtypeStruct((b, l, d), F32),
        grid=(b, l // tm, ne, fdim // tf),
        in_specs=in_specs,
        out_specs=tok,
        scratch_shapes=scratch,
        compiler_params=_params(("parallel", "parallel", "arbitrary", "arbitrary")),
        name="moe_ffn" if routed else "dense_ffn",
    )(*args)


def _rope_tables(s):
    half = DIFF_DK // 2
    n_freq = half // 2
    inv = ROPE_THETA ** (-jnp.arange(n_freq, dtype=F32) / n_freq)
    t = jnp.arange(s, dtype=jnp.int32)
    rows = (t // GRID_W).astype(F32)[:, None]
    cols = (t % GRID_W).astype(F32)[:, None]
    lane = jnp.arange(LANES, dtype=jnp.int32)
    dd = lane % DIFF_DK
    pos = jnp.where((dd < half)[None, :], rows, cols)
    ang = pos * inv[dd % n_freq][None, :]
    first = ((dd % half) < n_freq)[None, :]
    cos, sin = jnp.cos(ang), jnp.sin(ang)
    return cos, jnp.where(first, -sin, 0.0), jnp.where(first, 0.0, sin)


def _dft_tables(n):
    j = jnp.arange(n, dtype=jnp.int32)
    idx = (j[:, None] * j[None, :]) % n
    ang = idx.astype(F32) * (2.0 * math.pi / n)
    sc = n ** -0.5
    return jnp.cos(ang) * sc, jnp.sin(ang) * sc


def _channel_dft():
    c, s = _dft_tables(FNET_GROUP_CH)
    eye = jnp.eye(FNET_GROUPS, dtype=F32)
    return jnp.concatenate([jnp.kron(eye, c), jnp.kron(eye, s)], axis=1).astype(BF16)


def kernel(x, c, ctx, c_ctx, ada_w, ada_b, norm_mix_pre, norm_mix_post, norm_ffn_pre, norm_ffn_post,
           w_in, w_out, diff_lq1, diff_lk1, diff_lq2, diff_lk2, diff_subln, fnet_w,
           s5_a_re, s5_a_im, s5_log_dt, s5_b_re, s5_b_im, s5_c_re, s5_c_im, s5_d, s5_w_glu,
           ffn_w_gate, ffn_w_up, ffn_w_down, moe_router, moe_w_gate, moe_w_up, moe_w_down):
    b, s, d = x.shape
    lc = ctx.shape[1]
    depth = ada_w.shape[0]

    cc = jnp.zeros((MOD_ROWS, d), F32).at[:b].set(c).at[b].set(c_ctx)
    mod = _modulation(cc, ada_w, ada_b)

    rope_tabs = _rope_tables(s)
    cs64 = _channel_dft()
    dft = {n: tuple(t.astype(BF16) for t in _dft_tables(n)) for n in (s, lc)}

    xc = ctx
    for l in range(depth):
        need_ctx = l < depth - 1
        lam_init = 0.8 - 0.6 * math.exp(-0.3 * l)
        m_lat = [mod[l, :b, i * d:(i + 1) * d].reshape(b, 1, d) for i in range(6)]
        m_ctx = [jnp.broadcast_to(mod[l, b, i * d:(i + 1) * d].reshape(1, 1, d), (b, 1, d))
                 for i in range(6)]
        row = lambda v: v.reshape(1, -1).astype(F32)
        w_in_bf = w_in[l].astype(BF16)
        w_out_bf = w_out[l].astype(BF16)
        fw_bf = fnet_w[l].astype(BF16)
        wglu_bf = s5_w_glu[l].astype(BF16)
        lams = (row(diff_lq1[l]), row(diff_lk1[l]), row(diff_lq2[l]), row(diff_lk2[l]))
        subln = row(diff_subln[l])

        q, k, v, g1, g2, u = _inproj(x, m_lat[1], m_lat[0], row(norm_mix_pre[l]), w_in_bf, cs64,
                                     rope_tabs, min(512, s))
        qc, kc, vc, g1c, g2c, uc = _inproj(xc, m_ctx[1], m_ctx[0], row(norm_mix_pre[l]), w_in_bf,
                                           cs64, None, lc)
        a_lat = _attention(q, [(kc, vc), (k, v)], lams, subln, lam_init, min(256, s))
        f_lat = _fnet_dft(*dft[s], g1, g2, fw_bf, b, min(512, s), 2)

        mats = [_s5_matrices(s5_a_re[l, dr], s5_a_im[l, dr], s5_log_dt[l, dr], s5_b_re[l, dr],
                             s5_b_im[l, dr], s5_c_re[l, dr], s5_c_im[l, dr]) for dr in (0, 1)]
        mats = (mats[0][0], mats[1][0], mats[0][1], mats[1][1], mats[0][2], mats[1][2])
        zero_state = jnp.zeros((b, 2 * S5_LANES), F32)
        ycf, ycr, hcf, hcr = _s5_scan(uc.reshape(lc * b, S5_WIDTH), mats, zero_state, zero_state,
                                      b, 32)
        yf, yr, _, _ = _s5_scan(u.reshape(s * b, S5_WIDTH), mats, hcf, hcr, b, 32)
        tbv = lambda t, n: t.reshape(n, b * S5_WIDTH)

        x = _outproj(a_lat, f_lat, tbv(yf, s), tbv(yr, s), u, row(s5_d[l]), wglu_bf, w_out_bf,
                     x, m_lat[2], row(norm_mix_post[l]), min(512, s))
        if need_ctx:
            a_ctx = _attention(qc, [(kc, vc)], lams, subln, lam_init, lc)
            f_ctx = _fnet_dft(*dft[lc], g1c, g2c, fw_bf, b, lc, 2)
            xc = _outproj(a_ctx, f_ctx, tbv(ycf, lc), tbv(ycr, lc), uc, row(s5_d[l]), wglu_bf,
                          w_out_bf, xc, m_ctx[2], row(norm_mix_post[l]), lc)

        i = l // 2
        gpre, gpost = row(norm_ffn_pre[l]), row(norm_ffn_post[l])
        if l % 2 == 0:
            wg, wu, wd = (ffn_w_gate[i][None].astype(BF16), ffn_w_up[i][None].astype(BF16),
                          ffn_w_down[i][None].astype(BF16))
            router = None
            tf, fsub = 1408, 1408
        else:
            wg, wu, wd = (moe_w_gate[i].astype(BF16), moe_w_up[i].astype(BF16),
                          moe_w_down[i].astype(BF16))
            r = jnp.zeros((d, LANES), F32).at[:, :N_EXPERTS].set(moe_router[i])
            r_hi = r.astype(BF16)
            router = jnp.stack([r_hi, (r - r_hi.astype(F32)).astype(BF16)])
            tf, fsub = 896, 896
        x = _ffn(x, m_lat[4], m_lat[3], gpre, router, wg, wu, wd, m_lat[5], gpost,
                 min(1024, s), tf, fsub)
        if need_ctx:
            xc1 = _ffn(xc.reshape(1, b * lc, d), m_ctx[4][:1], m_ctx[3][:1], gpre, router, wg, wu, wd,
                       m_ctx[5][:1], gpost, min(1024, b * lc), tf, fsub)
            xc = xc1.reshape(b, lc, d)
    return x
```

```python
import functools
import math

import jax
import jax.numpy as jnp
from jax import lax
from jax.experimental import pallas as pl
from jax.experimental.pallas import tpu as pltpu

F32 = jnp.float32
BF16 = jnp.bfloat16

D_MODEL = 1024
DEPTH = 2
GRID_W = 64
EPS = 1e-6
DIFF_HEADS = 4
DIFF_DK = 64
DIFF_DV = 2 * DIFF_DK
DIFF_WIDTH = DIFF_HEADS * DIFF_DV
DIFF_QK_WIDTH = DIFF_HEADS * 2 * DIFF_DK
ROPE_THETA = 10000.0
FNET_GROUPS = 4
FNET_GROUP_CH = 64
FNET_WIDTH = FNET_GROUPS * FNET_GROUP_CH
S5_CH = 16
S5_GROUPS = 16
S5_STATE = 64
S5_WIDTH = S5_GROUPS * S5_CH
S5_LANES = S5_GROUPS * S5_STATE
IN_WIDTH = 2 * DIFF_QK_WIDTH + DIFF_WIDTH + FNET_WIDTH + S5_WIDTH
N_EXPERTS = 8
LANES = 128
MOD_ROWS = 24
FFN_DENSE_TF = 1408
MOE_TM = 1024
MOE_TF = 896

VMEM_LIMIT = 56 * 1024 * 1024


def _params(sem):
    return pltpu.CompilerParams(dimension_semantics=sem, vmem_limit_bytes=VMEM_LIMIT)


def _rms(x, g):
    return x * lax.rsqrt(jnp.mean(x * x, axis=-1, keepdims=True) + EPS) * g


def _dot(a, b):
    return jnp.dot(a, b, preferred_element_type=F32)


def _mod_kernel(c_ref, w_ref, b_ref, o_ref):
    c = c_ref[...]
    sc = c * jax.nn.sigmoid(c)
    o_ref[0] = jnp.dot(sc, w_ref[0], preferred_element_type=F32,
                       precision=lax.Precision.HIGHEST) + b_ref[0]


def _modulation(cc, ada_w, ada_b):
    depth, d, n = ada_w.shape
    tn = 1536
    return pl.pallas_call(
        _mod_kernel,
        out_shape=jax.ShapeDtypeStruct((depth, MOD_ROWS, n), F32),
        grid=(depth, n // tn),
        in_specs=[pl.BlockSpec((MOD_ROWS, d), lambda l, j: (0, 0)),
                  pl.BlockSpec((1, d, tn), lambda l, j: (l, 0, j)),
                  pl.BlockSpec((1, 1, tn), lambda l, j: (l, 0, j))],
        out_specs=pl.BlockSpec((1, MOD_ROWS, tn), lambda l, j: (l, 0, j)),
        compiler_params=_params(("parallel", "parallel")),
        name="adaln_mod",
    )(cc, ada_w, ada_b.reshape(depth, 1, n))


def _inproj_kernel(*refs, rope):
    if rope:
        (x_ref, sc_ref, sh_ref, g_ref, w_ref, cs_ref, cos_ref, sina_ref, sinb_ref,
         q_ref, k_ref, v_ref, g1_ref, g2_ref, u_ref) = refs
    else:
        (x_ref, sc_ref, sh_ref, g_ref, w_ref, cs_ref,
         q_ref, k_ref, v_ref, g1_ref, g2_ref, u_ref) = refs
    h = _rms(x_ref[0], g_ref[...]) * (1.0 + sc_ref[0]) + sh_ref[0]
    hb = h.astype(BF16)

    def proj(lo, hi):
        return _dot(hb, w_ref[:, lo:hi])

    def rotate(t):
        outs = []
        for j in range(t.shape[1] // LANES):
            tb = t[:, j * LANES:(j + 1) * LANES]
            outs.append(tb * cos_ref[...]
                        + pltpu.roll(tb, LANES - 16, axis=1) * sina_ref[...]
                        + pltpu.roll(tb, 16, axis=1) * sinb_ref[...])
        return jnp.concatenate(outs, axis=1)

    q = proj(0, DIFF_QK_WIDTH)
    k = proj(DIFF_QK_WIDTH, 2 * DIFF_QK_WIDTH)
    if rope:
        q = rotate(q)
        k = rotate(k)
    q_ref[0] = (q * (DIFF_DK ** -0.5)).astype(BF16)
    k_ref[0] = k.astype(BF16)
    o = 2 * DIFF_QK_WIDTH
    v_ref[0] = proj(o, o + DIFF_WIDTH).astype(BF16)
    o += DIFF_WIDTH
    f = proj(o, o + FNET_WIDTH).astype(BF16)
    g12 = _dot(f, cs_ref[...])
    g1_ref[...] = g12[:, :FNET_WIDTH].astype(BF16)
    g2_ref[...] = g12[:, FNET_WIDTH:].astype(BF16)
    u_ref[...] = proj(o + FNET_WIDTH, IN_WIDTH)


def _inproj(x, sc, sh, g, w_bf, cs64, rope_tabs, tl):
    b, l, d = x.shape
    rope = rope_tabs is not None
    in_specs = [pl.BlockSpec((1, tl, d), lambda i, bb: (bb, i, 0)),
                pl.BlockSpec((1, 1, d), lambda i, bb: (bb, 0, 0)),
                pl.BlockSpec((1, 1, d), lambda i, bb: (bb, 0, 0)),
                pl.BlockSpec((1, d), lambda i, bb: (0, 0)),
                pl.BlockSpec((d, IN_WIDTH), lambda i, bb: (0, 0)),
                pl.BlockSpec((FNET_WIDTH, 2 * FNET_WIDTH), lambda i, bb: (0, 0))]
    args = [x, sc, sh, g, w_bf, cs64]
    if rope:
        in_specs += [pl.BlockSpec((tl, LANES), lambda i, bb: (i, 0))] * 3
        args += list(rope_tabs)
    tok = lambda w: pl.BlockSpec((1, tl, w), lambda i, bb: (bb, i, 0))
    tb = pl.BlockSpec((tl, FNET_WIDTH), lambda i, bb: (i, bb))
    return pl.pallas_call(
        functools.partial(_inproj_kernel, rope=rope),
        out_shape=(jax.ShapeDtypeStruct((b, l, DIFF_QK_WIDTH), BF16),
                   jax.ShapeDtypeStruct((b, l, DIFF_QK_WIDTH), BF16),
                   jax.ShapeDtypeStruct((b, l, DIFF_WIDTH), BF16),
                   jax.ShapeDtypeStruct((l, b * FNET_WIDTH), BF16),
                   jax.ShapeDtypeStruct((l, b * FNET_WIDTH), BF16),
                   jax.ShapeDtypeStruct((l, b * S5_WIDTH), F32)),
        grid=(l // tl, b),
        in_specs=in_specs,
        out_specs=(tok(DIFF_QK_WIDTH), tok(DIFF_QK_WIDTH), tok(DIFF_WIDTH), tb, tb, tb),
        compiler_params=_params(("parallel", "parallel")),
        name="inproj_rope" if rope else "inproj",
    )(*args)


def _attn_kernel(*refs, n_src, lam_init):
    q_ref = refs[0]
    kv = refs[1:1 + 2 * n_src]
    lq1, lk1, lq2, lk2, sub_ref, o_ref, k1_s, k2_s, v_s = refs[1 + 2 * n_src:]

    @pl.when(pl.program_id(2) == 0)
    def _():
        off = 0
        for s in range(n_src):
            kk = kv[2 * s][0]
            n = kk.shape[0]
            k1_s[off:off + n, :] = kk[:, :DIFF_DK]
            k2_s[off:off + n, :] = kk[:, DIFF_DK:]
            v_s[off:off + n, :] = kv[2 * s + 1][0]
            off += n

    lam = (jnp.exp(jnp.sum(lq1[...] * lk1[...], axis=-1, keepdims=True))
           - jnp.exp(jnp.sum(lq2[...] * lk2[...], axis=-1, keepdims=True)) + lam_init)
    q = q_ref[0]

    def softmax_parts(qj, k_s):
        s = lax.dot_general(qj, k_s[...], (((1,), (1,)), ((), ())), preferred_element_type=F32)
        p = jnp.exp(s - jnp.max(s, axis=-1, keepdims=True))
        return p, jnp.sum(p, axis=-1, keepdims=True)

    p1, l1 = softmax_parts(q[:, :DIFF_DK], k1_s)
    p2, l2 = softmax_parts(q[:, DIFF_DK:], k2_s)
    p = p1 * (1.0 / l1) - p2 * (lam / l2)
    o = _dot(p.astype(BF16), v_s[...])
    o_ref[0] = (_rms(o, sub_ref[...]) * (1.0 - lam_init)).astype(BF16)


def _attention(q, kv_srcs, lams, subln, lam_init, tq):
    b, lq, _ = q.shape
    n_src = len(kv_srcs)
    lk = sum(k.shape[1] for k, _ in kv_srcs)
    in_specs = [pl.BlockSpec((1, tq, DIFF_DV), lambda bb, h, i: (bb, i, h))]
    args = [q]
    for k, v in kv_srcs:
        spec = pl.BlockSpec((1, k.shape[1], DIFF_DV), lambda bb, h, i: (bb, 0, h))
        in_specs += [spec, spec]
        args += [k, v]
    in_specs += [pl.BlockSpec((1, DIFF_DK), lambda bb, h, i: (0, 0))] * 4
    in_specs += [pl.BlockSpec((1, DIFF_DV), lambda bb, h, i: (0, 0))]
    args += list(lams) + [subln]
    return pl.pallas_call(
        functools.partial(_attn_kernel, n_src=n_src, lam_init=lam_init),
        out_shape=jax.ShapeDtypeStruct((b, lq, DIFF_WIDTH), BF16),
        grid=(b, DIFF_HEADS, lq // tq),
        in_specs=in_specs,
        out_specs=pl.BlockSpec((1, tq, DIFF_DV), lambda bb, h, i: (bb, i, h)),
        scratch_shapes=[pltpu.VMEM((lk, DIFF_DK), BF16), pltpu.VMEM((lk, DIFF_DK), BF16),
                        pltpu.VMEM((lk, DIFF_DV), BF16)],
        compiler_params=_params(("parallel", "parallel", "arbitrary")),
        name="diff_attn_%d" % n_src,
    )(*args)


def _dft_kernel(c_ref, s_ref, g1_ref, g2_ref, w_ref, o_ref):
    z = _dot(c_ref[...], g1_ref[...]) - _dot(s_ref[...], g2_ref[...])
    for j in range(o_ref.shape[0]):
        zj = z[:, j * FNET_WIDTH:(j + 1) * FNET_WIDTH].astype(BF16)
        o_ref[j] = _dot(zj, w_ref[...]).astype(BF16)


def _fnet_dft(cosm, sinm, g1, g2, w_bf, b, tm, nb):
    l = cosm.shape[0]
    tn = nb * FNET_WIDTH
    return pl.pallas_call(
        _dft_kernel,
        out_shape=jax.ShapeDtypeStruct((b, l, FNET_WIDTH), BF16),
        grid=(l // tm, b // nb),
        in_specs=[pl.BlockSpec((tm, l), lambda i, j: (i, 0)),
                  pl.BlockSpec((tm, l), lambda i, j: (i, 0)),
                  pl.BlockSpec((l, tn), lambda i, j: (0, j)),
                  pl.BlockSpec((l, tn), lambda i, j: (0, j)),
                  pl.BlockSpec((FNET_WIDTH, FNET_WIDTH), lambda i, j: (0, 0))],
        out_specs=pl.BlockSpec((nb, tm, FNET_WIDTH), lambda i, j: (j, i, 0)),
        compiler_params=_params(("parallel", "parallel")),
        name="fnet_dft",
    )(cosm, sinm, g1, g2, w_bf)


def _s5_kernel(uf_ref, ur_ref, bf_ref, br_ref, lf_ref, lr_ref, cf_ref, cr_ref, h0f_ref, h0r_ref,
               yf_ref, yr_ref, hef_ref, her_ref, hs_f, hs_r, drv, hbuf, *, tc, nb):
    j = pl.program_id(0)

    @pl.when(j == 0)
    def _():
        hs_f[...] = h0f_ref[...]
        hs_r[...] = h0r_ref[...]

    def run(u_ref, b_ref, l_ref, c_ref, hs, y_ref, reverse):
        drv[...] = _dot(u_ref[...].astype(BF16), b_ref[...])

        def step(t, carry):
            hr, hi = carry
            lre = l_ref[:, :S5_LANES]
            lim = l_ref[:, S5_LANES:]
            tt = tc - 1 - t if reverse else t
            rows = pl.ds(pl.multiple_of(tt * nb, nb), nb)
            d = drv[rows, :]
            nr = lre * hr - lim * hi + d[:, :S5_LANES]
            ni = lre * hi + lim * hr + d[:, S5_LANES:]
            hbuf[rows, :S5_LANES] = nr.astype(BF16)
            hbuf[rows, S5_LANES:] = ni.astype(BF16)
            return nr, ni

        hr, hi = lax.fori_loop(0, tc, step, (hs[:, :S5_LANES], hs[:, S5_LANES:]))
        hs[:, :S5_LANES] = hr
        hs[:, S5_LANES:] = hi
        y_ref[...] = _dot(hbuf[...], c_ref[...])

    run(uf_ref, bf_ref, lf_ref, cf_ref, hs_f, yf_ref, False)
    run(ur_ref, br_ref, lr_ref, cr_ref, hs_r, yr_ref, True)

    @pl.when(j == pl.num_programs(0) - 1)
    def _():
        hef_ref[...] = hs_f[...]
        her_ref[...] = hs_r[...]


def _s5_scan(u_tb, mats, h0f, h0r, nb, tc):
    rows = u_tb.shape[0]
    n = rows // (tc * nb)
    r = tc * nb
    bmf, bmr, lf, lr, cmf, cmr = mats
    full = lambda a: pl.BlockSpec(a.shape, lambda j: (0,) * a.ndim)
    fwd = pl.BlockSpec((r, S5_WIDTH), lambda j: (j, 0))
    rev = pl.BlockSpec((r, S5_WIDTH), lambda j: (n - 1 - j, 0))
    st = jax.ShapeDtypeStruct((nb, 2 * S5_LANES), F32)
    return pl.pallas_call(
        functools.partial(_s5_kernel, tc=tc, nb=nb),
        out_shape=(jax.ShapeDtypeStruct((rows, S5_WIDTH), F32),
                   jax.ShapeDtypeStruct((rows, S5_WIDTH), F32), st, st),
        grid=(n,),
        in_specs=[fwd, rev, full(bmf), full(bmr), full(lf), full(lr), full(cmf), full(cmr),
                  full(h0f), full(h0r)],
        out_specs=(fwd, rev, full(h0f), full(h0r)),
        scratch_shapes=[pltpu.VMEM((nb, 2 * S5_LANES), F32), pltpu.VMEM((nb, 2 * S5_LANES), F32),
                        pltpu.VMEM((r, 2 * S5_LANES), F32), pltpu.VMEM((r, 2 * S5_LANES), BF16)],
        compiler_params=_params(("arbitrary",)),
        name="s5_scan",
    )(u_tb, u_tb, bmf, bmr, lf, lr, cmf, cmr, h0f, h0r)


def _s5_matrices(a_re, a_im, log_dt, b_re, b_im, c_re, c_im):
    dt = jnp.exp(log_dt)[:, None]
    mag = jnp.exp(a_re * dt)
    lr, li = mag * jnp.cos(a_im * dt), mag * jnp.sin(a_im * dt)
    nr, ni = lr - 1.0, li
    den = a_re * a_re + a_im * a_im
    cr = (nr * a_re + ni * a_im) / den
    ci = (ni * a_re - nr * a_im) / den
    bbr = cr[..., None] * b_re - ci[..., None] * b_im
    bbi = cr[..., None] * b_im + ci[..., None] * b_re
    eye = jnp.eye(S5_GROUPS, dtype=F32)

    def drive_mat(bb):
        return jnp.einsum('gpc,gh->gchp', bb, eye).reshape(S5_WIDTH, S5_LANES)

    def read_mat(cc):
        return jnp.einsum('gcp,gh->gphc', cc, eye).reshape(S5_LANES, S5_WIDTH)

    bm = jnp.concatenate([drive_mat(bbr), drive_mat(bbi)], axis=1).astype(BF16)
    cm = jnp.concatenate([read_mat(c_re), -read_mat(c_im)], axis=0).astype(BF16)
    lam = jnp.concatenate([lr.reshape(1, S5_LANES), li.reshape(1, S5_LANES)], axis=1)
    return bm, lam, cm


def _outproj_kernel(a_ref, f_ref, yf_ref, yr_ref, u_ref, d_ref, wglu_ref, wa_ref, wf_ref, ws_ref,
                    x_ref, gate_ref, g_ref, o_ref):
    yy = yf_ref[...] + yr_ref[...] + d_ref[...] * u_ref[...]
    yy = jax.nn.gelu(yy)
    s = yy * jax.nn.sigmoid(_dot(yy.astype(BF16), wglu_ref[...]))
    mix = (_dot(a_ref[0], wa_ref[...]) + _dot(f_ref[0], wf_ref[...])
           + _dot(s.astype(BF16), ws_ref[...]))
    o_ref[0] = x_ref[0] + gate_ref[0] * _rms(mix, g_ref[...])


def _outproj(a, fo, yf, yr, u, d, wglu, w_out, x, gate, g, tl):
    b, l, dm = x.shape
    wa, wf, ws = (w_out[:DIFF_WIDTH], w_out[DIFF_WIDTH:DIFF_WIDTH + FNET_WIDTH],
                  w_out[DIFF_WIDTH + FNET_WIDTH:])
    tok = lambda w: pl.BlockSpec((1, tl, w), lambda i, bb: (bb, i, 0))
    tb = pl.BlockSpec((tl, S5_WIDTH), lambda i, bb: (i, bb))
    full = lambda arr: pl.BlockSpec(arr.shape, lambda i, bb: (0,) * arr.ndim)
    return pl.pallas_call(
        _outproj_kernel,
        out_shape=jax.ShapeDtypeStruct((b, l, dm), F32),
        grid=(l // tl, b),
        in_specs=[tok(DIFF_WIDTH), tok(FNET_WIDTH), tb, tb, tb, full(d), full(wglu),
                  full(wa), full(wf), full(ws), tok(dm),
                  pl.BlockSpec((1, 1, dm), lambda i, bb: (bb, 0, 0)), full(g)],
        out_specs=tok(dm),
        compiler_params=_params(("parallel", "parallel")),
        name="outproj",
    )(a, fo, yf, yr, u, d, wglu, wa, wf, ws, x, gate, g)


def _ffn_kernel(x_ref, sc_ref, sh_ref, g_ref, wg_ref, wu_ref, wd_ref, gate_ref, gp_ref,
                o_ref, h_s, acc_s):
    f = pl.program_id(2)

    @pl.when(f == 0)
    def _():
        h = _rms(x_ref[0], g_ref[...]) * (1.0 + sc_ref[0]) + sh_ref[0]
        h_s[...] = h.astype(BF16)
        acc_s[...] = jnp.zeros_like(acc_s)

    hb = h_s[...]
    gt = _dot(hb, wg_ref[...])
    up = _dot(hb, wu_ref[...])
    act = (gt * jax.nn.sigmoid(gt) * up).astype(BF16)
    acc_s[...] += _dot(act, wd_ref[...])

    @pl.when(f == pl.num_programs(2) - 1)
    def _():
        o_ref[0] = x_ref[0] + gate_ref[0] * _rms(acc_s[...], gp_ref[...])


def _ffn(x, sc, sh, g, wg, wu, wd, gate, gp, tm, tf):
    b, l, d = x.shape
    fdim = wg.shape[1]
    tok = pl.BlockSpec((1, tm, d), lambda bb, i, f: (bb, i, 0))
    modv = pl.BlockSpec((1, 1, d), lambda bb, i, f: (bb, 0, 0))
    vec = pl.BlockSpec((1, d), lambda bb, i, f: (0, 0))
    return pl.pallas_call(
        _ffn_kernel,
        out_shape=jax.ShapeDtypeStruct((b, l, d), F32),
        grid=(b, l // tm, fdim // tf),
        in_specs=[tok, modv, modv, vec,
                  pl.BlockSpec((d, tf), lambda bb, i, f: (0, f)),
                  pl.BlockSpec((d, tf), lambda bb, i, f: (0, f)),
                  pl.BlockSpec((tf, d), lambda bb, i, f: (f, 0)),
                  modv, vec],
        out_specs=tok,
        scratch_shapes=[pltpu.VMEM((tm, d), BF16), pltpu.VMEM((tm, d), F32)],
        compiler_params=_params(("parallel", "parallel", "arbitrary")),
        name="dense_ffn",
    )(x, sc, sh, g, wg, wu, wd, gate, gp)


def _router_kernel(x_ref, sc_ref, sh_ref, g_ref, r_ref, h_ref, info_ref):
    h = _rms(x_ref[0], g_ref[...]) * (1.0 + sc_ref[0]) + sh_ref[0]
    h_ref[0] = h
    hi = h.astype(BF16)
    lo = (h - hi.astype(F32)).astype(BF16)
    logits = _dot(hi, r_ref[0]) + _dot(hi, r_ref[1]) + _dot(lo, r_ref[0])
    lane = lax.broadcasted_iota(jnp.int32, logits.shape, 1).astype(F32)
    neg = jnp.float32(-jnp.inf)
    logits = jnp.where(lane < N_EXPERTS, logits, neg)
    m1 = jnp.max(logits, axis=-1, keepdims=True)
    i1 = jnp.min(jnp.where(logits == m1, lane, float(LANES)), axis=-1, keepdims=True)
    rest = jnp.where(lane == i1, neg, logits)
    m2 = jnp.max(rest, axis=-1, keepdims=True)
    i2 = jnp.min(jnp.where(rest == m2, lane, float(LANES)), axis=-1, keepdims=True)
    e2 = jnp.exp(m2 - m1)
    w1 = 1.0 / (1.0 + e2)
    info_ref[0] = jnp.where(lane == 0, i1, jnp.where(lane == 1, i2, jnp.where(
        lane == 2, w1, jnp.where(lane == 3, e2 * w1, 0.0))))


def _router(x, sc, sh, g, router, tm):
    b, l, d = x.shape
    tok = pl.BlockSpec((1, tm, d), lambda bb, i: (bb, i, 0))
    modv = pl.BlockSpec((1, 1, d), lambda bb, i: (bb, 0, 0))
    return pl.pallas_call(
        _router_kernel,
        out_shape=(jax.ShapeDtypeStruct((b, l, d), F32), jax.ShapeDtypeStruct((b, l, LANES), F32)),
        grid=(b, l // tm),
        in_specs=[tok, modv, modv, pl.BlockSpec((1, d), lambda bb, i: (0, 0)),
                  pl.BlockSpec(router.shape, lambda bb, i: (0, 0, 0))],
        out_specs=(tok, pl.BlockSpec((1, tm, LANES), lambda bb, i: (bb, i, 0))),
        compiler_params=_params(("parallel", "parallel")),
        name="moe_router",
    )(x, sc, sh, g, router)


def _routing_tables(ids, wts, tm, n_tiles):
    n2 = ids.size
    e = ids.reshape(n2)
    onehot = (e[:, None] == jnp.arange(N_EXPERTS, dtype=jnp.int32)[None, :]).astype(jnp.int32)
    csum = jnp.cumsum(onehot, axis=0)
    rank = jnp.take_along_axis(csum, e[:, None], axis=1)[:, 0] - 1
    counts = csum[-1]
    padded = ((counts + tm - 1) // tm) * tm
    ends = jnp.cumsum(padded)
    starts = ends - padded
    dest = (starts[e] + rank).astype(jnp.int32)
    tile_row0 = jnp.arange(n_tiles, dtype=jnp.int32) * tm
    tile_e = jnp.minimum(jnp.sum((ends[None, :] <= tile_row0[:, None]).astype(jnp.int32), axis=1),
                         N_EXPERTS - 1).astype(jnp.int32)
    rows = n_tiles * tm
    src = jnp.zeros((rows,), jnp.int32).at[dest].set(jnp.arange(n2, dtype=jnp.int32) // 2)
    wsort = jnp.zeros((rows,), F32).at[dest].set(wts.reshape(n2))
    n_live_tiles = (ends[-1] // tm).astype(jnp.int32)
    return src, wsort, tile_e, n_live_tiles, dest


def _gffn_kernel(te_ref, nv_ref, cur_ref, nxt_ref, h_hbm, wg_ref, wu_ref, wd_ref, ws_ref,
                 o_ref, xbuf, hb_s, acc_s, sem, *, tm, nf):
    t = pl.program_id(0)
    f = pl.program_id(1)
    n_live = nv_ref[0]
    slot = t % 2
    chunk = tm // nf

    def row_copy(idx_ref, r, dst_slot):
        return pltpu.make_async_copy(h_hbm.at[pl.ds(idx_ref[0, 0, r], 1)],
                                     xbuf.at[dst_slot, pl.ds(r, 1)], sem.at[dst_slot])

    def wait_tile(dst_slot):
        pltpu.make_async_copy(h_hbm.at[pl.ds(0, tm)], xbuf.at[dst_slot], sem.at[dst_slot]).wait()

    @pl.when((t == 0) & (f == 0))
    def _():
        def body(r, c):
            row_copy(cur_ref, r, 0).start()
            return c
        lax.fori_loop(0, tm, body, 0)

    @pl.when(t < n_live)
    def _():
        @pl.when(f == 0)
        def _():
            wait_tile(slot)
            hb_s[...] = xbuf[slot].astype(BF16)
            acc_s[...] = jnp.zeros_like(acc_s)

        for j in range(chunk):
            row_copy(nxt_ref, f * chunk + j, 1 - slot).start()

        hb = hb_s[...]
        gt = _dot(hb, wg_ref[0])
        up = _dot(hb, wu_ref[0])
        act = (gt * jax.nn.sigmoid(gt) * up).astype(BF16)
        acc_s[...] += _dot(act, wd_ref[0])

        @pl.when(f == nf - 1)
        def _():
            o_ref[...] = acc_s[...] * ws_ref[...]

        @pl.when((f == nf - 1) & (t == n_live - 1))
        def _():
            wait_tile(1 - slot)

    @pl.when((t >= n_live) & (f == nf - 1))
    def _():
        o_ref[...] = jnp.zeros_like(o_ref)


def _grouped_ffn(h2d, src, wsort, tile_e, n_live, wg, wu, wd, tm, tf):
    n, d = h2d.shape
    n_tiles = tile_e.shape[0]
    fdim = wg.shape[2]
    nf = fdim // tf
    src3 = src.reshape(n_tiles, 1, tm)
    smem_blk = lambda off: pl.BlockSpec(
        (1, 1, tm), lambda t, f, te, nv: (jnp.minimum(t + off, n_tiles - 1), 0, 0),
        memory_space=pltpu.SMEM)
    grid_spec = pltpu.PrefetchScalarGridSpec(
        num_scalar_prefetch=2,
        grid=(n_tiles, nf),
        in_specs=[smem_blk(0), smem_blk(1),
                  pl.BlockSpec(memory_space=pl.ANY),
                  pl.BlockSpec((1, d, tf), lambda t, f, te, nv: (te[t], 0, f)),
                  pl.BlockSpec((1, d, tf), lambda t, f, te, nv: (te[t], 0, f)),
                  pl.BlockSpec((1, tf, d), lambda t, f, te, nv: (te[t], f, 0)),
                  pl.BlockSpec((tm, 1), lambda t, f, te, nv: (t, 0))],
        out_specs=pl.BlockSpec((tm, d), lambda t, f, te, nv: (t, 0)),
        scratch_shapes=[pltpu.VMEM((2, tm, d), F32), pltpu.VMEM((tm, d), BF16),
                        pltpu.VMEM((tm, d), F32), pltpu.SemaphoreType.DMA((2,))])
    return pl.pallas_call(
        functools.partial(_gffn_kernel, tm=tm, nf=nf),
        out_shape=jax.ShapeDtypeStruct((n_tiles * tm, d), F32),
        grid_spec=grid_spec,
        compiler_params=_params(("arbitrary", "arbitrary")),
        name="moe_grouped_ffn",
    )(tile_e, n_live.reshape(1), src3, src3, h2d, wg, wu, wd, wsort.reshape(n_tiles * tm, 1))


def _combine_kernel(d0_ref, d1_ref, ys_hbm, x_ref, gate_ref, gp_ref, o_ref, b0, b1, sem, *, tc):
    def body(r, c):
        pltpu.make_async_copy(ys_hbm.at[pl.ds(d0_ref[0, 0, r], 1)], b0.at[pl.ds(r, 1)], sem.at[0]).start()
        pltpu.make_async_copy(ys_hbm.at[pl.ds(d1_ref[0, 0, r], 1)], b1.at[pl.ds(r, 1)], sem.at[1]).start()
        return c
    lax.fori_loop(0, tc, body, 0)
    pltpu.make_async_copy(ys_hbm.at[pl.ds(0, tc)], b0, sem.at[0]).wait()
    pltpu.make_async_copy(ys_hbm.at[pl.ds(0, tc)], b1, sem.at[1]).wait()
    o_ref[0] = x_ref[0] + gate_ref[0] * _rms(b0[...] + b1[...], gp_ref[...])


def _combine(ys, dest, x, gate, gp, tc):
    b, l, d = x.shape
    nt = l // tc
    dd = dest.reshape(b * nt, tc, 2)
    d0 = dd[:, :, 0].reshape(b * nt, 1, tc)
    d1 = dd[:, :, 1].reshape(b * nt, 1, tc)
    smem = pl.BlockSpec((1, 1, tc), lambda bb, i: (bb * nt + i, 0, 0), memory_space=pltpu.SMEM)
    tok = pl.BlockSpec((1, tc, d), lambda bb, i: (bb, i, 0))
    return pl.pallas_call(
        functools.partial(_combine_kernel, tc=tc),
        out_shape=jax.ShapeDtypeStruct((b, l, d), F32),
        grid=(b, nt),
        in_specs=[smem, smem, pl.BlockSpec(memory_space=pl.ANY), tok,
                  pl.BlockSpec((1, 1, d), lambda bb, i: (bb, 0, 0)),
                  pl.BlockSpec((1, d), lambda bb, i: (0, 0))],
        out_specs=tok,
        scratch_shapes=[pltpu.VMEM((tc, d), F32), pltpu.VMEM((tc, d), F32),
                        pltpu.SemaphoreType.DMA((2,))],
        compiler_params=_params(("arbitrary", "arbitrary")),
        name="moe_combine",
    )(d0, d1, ys, x, gate, gp)


def _moe(x, sc, sh, gpre, router, wg, wu, wd, gate, gpost, tm, tf):
    b, l, d = x.shape
    n = b * l
    h, info = _router(x, sc, sh, gpre, router, min(1024, l))
    info = info.reshape(n, LANES)
    ids = info[:, :2].astype(jnp.int32)
    wts = info[:, 2:4]
    n_tiles = (2 * n) // tm + N_EXPERTS
    src, wsort, tile_e, n_live, dest = _routing_tables(ids, wts, tm, n_tiles)
    ys = _grouped_ffn(h.reshape(n, d), src, wsort, tile_e, n_live, wg, wu, wd, tm, tf)
    return _combine(ys, dest, x, gate, gpost, min(512, l))


def _rope_tables(s):
    half = DIFF_DK // 2
    n_freq = half // 2
    inv = ROPE_THETA ** (-jnp.arange(n_freq, dtype=F32) / n_freq)
    t = jnp.arange(s, dtype=jnp.int32)
    rows = (t // GRID_W).astype(F32)[:, None]
    cols = (t % GRID_W).astype(F32)[:, None]
    lane = jnp.arange(LANES, dtype=jnp.int32)
    dd = lane % DIFF_DK
    pos = jnp.where((dd < half)[None, :], rows, cols)
    ang = pos * inv[dd % n_freq][None, :]
    first = ((dd % half) < n_freq)[None, :]
    cos, sin = jnp.cos(ang), jnp.sin(ang)
    return cos, jnp.where(first, -sin, 0.0), jnp.where(first, 0.0, sin)


def _dft_tables(n):
    j = jnp.arange(n, dtype=jnp.int32)
    idx = (j[:, None] * j[None, :]) % n
    ang = idx.astype(F32) * (2.0 * math.pi / n)
    sc = n ** -0.5
    return jnp.cos(ang) * sc, jnp.sin(ang) * sc


def _channel_dft():
    c, s = _dft_tables(FNET_GROUP_CH)
    eye = jnp.eye(FNET_GROUPS, dtype=F32)
    return jnp.concatenate([jnp.kron(eye, c), jnp.kron(eye, s)], axis=1).astype(BF16)


def kernel(x, c, ctx, c_ctx, ada_w, ada_b, norm_mix_pre, norm_mix_post, norm_ffn_pre, norm_ffn_post,
           w_in, w_out, diff_lq1, diff_lk1, diff_lq2, diff_lk2, diff_subln, fnet_w,
           s5_a_re, s5_a_im, s5_log_dt, s5_b_re, s5_b_im, s5_c_re, s5_c_im, s5_d, s5_w_glu,
           ffn_w_gate, ffn_w_up, ffn_w_down, moe_router, moe_w_gate, moe_w_up, moe_w_down):
    b, s, d = x.shape
    lc = ctx.shape[1]
    depth = ada_w.shape[0]

    cc = jnp.zeros((MOD_ROWS, d), F32).at[:b].set(c).at[b].set(c_ctx)
    mod = _modulation(cc, ada_w, ada_b)

    rope_tabs = _rope_tables(s)
    cs64 = _channel_dft()
    dft = {n: tuple(t.astype(BF16) for t in _dft_tables(n)) for n in (s, lc)}

    xc = ctx
    for l in range(depth):
        need_ctx = l < depth - 1
        lam_init = 0.8 - 0.6 * math.exp(-0.3 * l)
        m_lat = [mod[l, :b, i * d:(i + 1) * d].reshape(b, 1, d) for i in range(6)]
        m_ctx = [jnp.broadcast_to(mod[l, b, i * d:(i + 1) * d].reshape(1, 1, d), (b, 1, d))
                 for i in range(6)]
        row = lambda v: v.reshape(1, -1).astype(F32)
        w_in_bf = w_in[l].astype(BF16)
        w_out_bf = w_out[l].astype(BF16)
        fw_bf = fnet_w[l].astype(BF16)
        wglu_bf = s5_w_glu[l].astype(BF16)
        lams = (row(diff_lq1[l]), row(diff_lk1[l]), row(diff_lq2[l]), row(diff_lk2[l]))
        subln = row(diff_subln[l])

        q, k, v, g1, g2, u = _inproj(x, m_lat[1], m_lat[0], row(norm_mix_pre[l]), w_in_bf, cs64,
                                     rope_tabs, min(512, s))
        qc, kc, vc, g1c, g2c, uc = _inproj(xc, m_ctx[1], m_ctx[0], row(norm_mix_pre[l]), w_in_bf,
                                           cs64, None, lc)
        a_lat = _attention(q, [(kc, vc), (k, v)], lams, subln, lam_init, min(256, s))
        f_lat = _fnet_dft(*dft[s], g1, g2, fw_bf, b, min(512, s), 2)

        mats = [_s5_matrices(s5_a_re[l, dr], s5_a_im[l, dr], s5_log_dt[l, dr], s5_b_re[l, dr],
                             s5_b_im[l, dr], s5_c_re[l, dr], s5_c_im[l, dr]) for dr in (0, 1)]
        mats = (mats[0][0], mats[1][0], mats[0][1], mats[1][1], mats[0][2], mats[1][2])
        zero_state = jnp.zeros((b, 2 * S5_LANES), F32)
        ycf, ycr, hcf, hcr = _s5_scan(uc.reshape(lc * b, S5_WIDTH), mats, zero_state, zero_state,
                                      b, 32)
        yf, yr, _, _ = _s5_scan(u.reshape(s * b, S5_WIDTH), mats, hcf, hcr, b, 32)
        tbv = lambda t, n: t.reshape(n, b * S5_WIDTH)

        x = _outproj(a_lat, f_lat, tbv(yf, s), tbv(yr, s), u, row(s5_d[l]), wglu_bf, w_out_bf,
                     x, m_lat[2], row(norm_mix_post[l]), min(512, s))
        if need_ctx:
            a_ctx = _attention(qc, [(kc, vc)], lams, subln, lam_init, lc)
            f_ctx = _fnet_dft(*dft[lc], g1c, g2c, fw_bf, b, lc, 2)
            xc = _outproj(a_ctx, f_ctx, tbv(ycf, lc), tbv(ycr, lc), uc, row(s5_d[l]), wglu_bf,
                          w_out_bf, xc, m_ctx[2], row(norm_mix_post[l]), lc)

        i = l // 2
        gpre, gpost = row(norm_ffn_pre[l]), row(norm_ffn_post[l])
        xc1 = xc.reshape(1, b * lc, d)
        if l % 2 == 0:
            wg, wu, wd = (ffn_w_gate[i].astype(BF16), ffn_w_up[i].astype(BF16),
                          ffn_w_down[i].astype(BF16))
            tf = FFN_DENSE_TF
            x = _ffn(x, m_lat[4], m_lat[3], gpre, wg, wu, wd, m_lat[5], gpost, min(1024, s), tf)
            if need_ctx:
                xc1 = _ffn(xc1, m_ctx[4][:1], m_ctx[3][:1], gpre, wg, wu, wd, m_ctx[5][:1], gpost,
                           min(1024, b * lc), tf)
        else:
            wg, wu, wd = (moe_w_gate[i].astype(BF16), moe_w_up[i].astype(BF16),
                          moe_w_down[i].astype(BF16))
            r = jnp.zeros((d, LANES), F32).at[:, :N_EXPERTS].set(moe_router[i])
            r_hi = r.astype(BF16)
            router = jnp.stack([r_hi, (r - r_hi.astype(F32)).astype(BF16)])
            x = _moe(x, m_lat[4], m_lat[3], gpre, router, wg, wu, wd, m_lat[5], gpost,
                     MOE_TM, MOE_TF)
            if need_ctx:
                xc1 = _moe(xc1, m_ctx[4][:1], m_ctx[3][:1], gpre, router, wg, wu, wd, m_ctx[5][:1],
                           gpost, MOE_TM, MOE_TF)
        xc = xc1.reshape(b, lc, d)
    return x
```

```python
import functools
import math

import jax
import jax.numpy as jnp
from jax import lax
from jax.experimental import pallas as pl
from jax.experimental.pallas import tpu as pltpu

F32 = jnp.float32
BF16 = jnp.bfloat16

D_MODEL = 1024
DEPTH = 2
GRID_W = 64
EPS = 1e-6
DIFF_HEADS = 4
DIFF_DK = 64
DIFF_DV = 2 * DIFF_DK
DIFF_WIDTH = DIFF_HEADS * DIFF_DV
DIFF_QK_WIDTH = DIFF_HEADS * 2 * DIFF_DK
ROPE_THETA = 10000.0
FNET_GROUPS = 4
FNET_GROUP_CH = 64
FNET_WIDTH = FNET_GROUPS * FNET_GROUP_CH
S5_CH = 16
S5_GROUPS = 16
S5_STATE = 64
S5_WIDTH = S5_GROUPS * S5_CH
S5_LANES = S5_GROUPS * S5_STATE
IN_WIDTH = 2 * DIFF_QK_WIDTH + DIFF_WIDTH + FNET_WIDTH + S5_WIDTH
N_EXPERTS = 8
LANES = 128
LOG2E = math.log2(math.e)
MOD_ROWS = 24
FFN_DENSE_TF = 1408
ATTN_RSUB = 128
S5_TC = 32
MOE_TM = 1024
MOE_TF = 896

VMEM_LIMIT = 56 * 1024 * 1024


def _params(sem):
    return pltpu.CompilerParams(dimension_semantics=sem, vmem_limit_bytes=VMEM_LIMIT)


def _rms(x, g):
    return x * lax.rsqrt(jnp.mean(x * x, axis=-1, keepdims=True) + EPS) * g


def _dot(a, b):
    return jnp.dot(a, b, preferred_element_type=F32)


def _mod_kernel(c_ref, w_ref, b_ref, o_ref):
    c = c_ref[...]
    sc = c * jax.nn.sigmoid(c)
    o_ref[0] = jnp.dot(sc, w_ref[0], preferred_element_type=F32,
                       precision=lax.Precision.HIGHEST) + b_ref[0]


def _modulation(cc, ada_w, ada_b):
    depth, d, n = ada_w.shape
    tn = 1536
    return pl.pallas_call(
        _mod_kernel,
        out_shape=jax.ShapeDtypeStruct((depth, MOD_ROWS, n), F32),
        grid=(depth, n // tn),
        in_specs=[pl.BlockSpec((MOD_ROWS, d), lambda l, j: (0, 0)),
                  pl.BlockSpec((1, d, tn), lambda l, j: (l, 0, j)),
                  pl.BlockSpec((1, 1, tn), lambda l, j: (l, 0, j))],
        out_specs=pl.BlockSpec((1, MOD_ROWS, tn), lambda l, j: (l, 0, j)),
        compiler_params=_params(("parallel", "parallel")),
        name="adaln_mod",
    )(cc, ada_w, ada_b.reshape(depth, 1, n))


def _inproj_kernel(*refs, rope):
    if rope:
        (x_ref, sc_ref, sh_ref, g_ref, w_ref, cs_ref, cos_ref, sina_ref, sinb_ref,
         q_ref, k_ref, v_ref, g1_ref, g2_ref, u_ref) = refs
    else:
        (x_ref, sc_ref, sh_ref, g_ref, w_ref, cs_ref,
         q_ref, k_ref, v_ref, g1_ref, g2_ref, u_ref) = refs
    h = _rms(x_ref[0], g_ref[...]) * (1.0 + sc_ref[0]) + sh_ref[0]
    hb = h.astype(BF16)

    def proj(lo, hi):
        return _dot(hb, w_ref[:, lo:hi])

    def rotate(t):
        outs = []
        for j in range(t.shape[1] // LANES):
            tb = t[:, j * LANES:(j + 1) * LANES]
            outs.append(tb * cos_ref[...]
                        + pltpu.roll(tb, LANES - 16, axis=1) * sina_ref[...]
                        + pltpu.roll(tb, 16, axis=1) * sinb_ref[...])
        return jnp.concatenate(outs, axis=1)

    q = proj(0, DIFF_QK_WIDTH)
    k = proj(DIFF_QK_WIDTH, 2 * DIFF_QK_WIDTH)
    if rope:
        q = rotate(q)
        k = rotate(k)
    q_ref[0] = (q * (DIFF_DK ** -0.5 * LOG2E)).astype(BF16)
    k_ref[0] = k.astype(BF16)
    o = 2 * DIFF_QK_WIDTH
    v_ref[0] = proj(o, o + DIFF_WIDTH).astype(BF16)
    o += DIFF_WIDTH
    f = proj(o, o + FNET_WIDTH).astype(BF16)
    g12 = _dot(f, cs_ref[...])
    g1_ref[...] = g12[:, :FNET_WIDTH].astype(BF16)
    g2_ref[...] = g12[:, FNET_WIDTH:].astype(BF16)
    u_ref[...] = proj(o + FNET_WIDTH, IN_WIDTH)


def _inproj(x, sc, sh, g, w_bf, cs64, rope_tabs, tl):
    b, l, d = x.shape
    rope = rope_tabs is not None
    in_specs = [pl.BlockSpec((1, tl, d), lambda i, bb: (bb, i, 0)),
                pl.BlockSpec((1, 1, d), lambda i, bb: (bb, 0, 0)),
                pl.BlockSpec((1, 1, d), lambda i, bb: (bb, 0, 0)),
                pl.BlockSpec((1, d), lambda i, bb: (0, 0)),
                pl.BlockSpec((d, IN_WIDTH), lambda i, bb: (0, 0)),
                pl.BlockSpec((FNET_WIDTH, 2 * FNET_WIDTH), lambda i, bb: (0, 0))]
    args = [x, sc, sh, g, w_bf, cs64]
    if rope:
        in_specs += [pl.BlockSpec((tl, LANES), lambda i, bb: (i, 0))] * 3
        args += list(rope_tabs)
    tok = lambda w: pl.BlockSpec((1, tl, w), lambda i, bb: (bb, i, 0))
    tb = pl.BlockSpec((tl, FNET_WIDTH), lambda i, bb: (i, bb))
    return pl.pallas_call(
        functools.partial(_inproj_kernel, rope=rope),
        out_shape=(jax.ShapeDtypeStruct((b, l, DIFF_QK_WIDTH), BF16),
                   jax.ShapeDtypeStruct((b, l, DIFF_QK_WIDTH), BF16),
                   jax.ShapeDtypeStruct((b, l, DIFF_WIDTH), BF16),
                   jax.ShapeDtypeStruct((l, b * FNET_WIDTH), BF16),
                   jax.ShapeDtypeStruct((l, b * FNET_WIDTH), BF16),
                   jax.ShapeDtypeStruct((l, b * S5_WIDTH), F32)),
        grid=(l // tl, b),
        in_specs=in_specs,
        out_specs=(tok(DIFF_QK_WIDTH), tok(DIFF_QK_WIDTH), tok(DIFF_WIDTH), tb, tb, tb),
        compiler_params=_params(("parallel", "parallel")),
        name="inproj_rope" if rope else "inproj",
    )(*args)


def _attn_kernel(*refs, n_src, lam_init, rsub):
    q_ref = refs[0]
    kv = refs[1:1 + 2 * n_src]
    lq1, lk1, lq2, lk2, sub_ref, o_ref, k1_s, k2_s, v_s = refs[1 + 2 * n_src:]

    @pl.when(pl.program_id(2) == 0)
    def _():
        off = 0
        for s in range(n_src):
            kk = kv[2 * s][0]
            n = kk.shape[0]
            k1_s[off:off + n, :] = kk[:, :DIFF_DK]
            k2_s[off:off + n, :] = kk[:, DIFF_DK:]
            v_s[off:off + n, :DIFF_DV] = kv[2 * s + 1][0]
            off += n
        v_s[:, DIFF_DV:] = jnp.ones((v_s.shape[0], DIFF_DV), BF16)

    lam = (jnp.exp(jnp.sum(lq1[...] * lk1[...], axis=-1, keepdims=True))
           - jnp.exp(jnp.sum(lq2[...] * lk2[...], axis=-1, keepdims=True)) + lam_init)
    def attend(qj, k_s):
        s = lax.dot_general(qj, k_s[...], (((1,), (1,)), ((), ())), preferred_element_type=F32)
        p = jnp.exp2(s - jnp.max(s, axis=-1, keepdims=True)).astype(BF16)
        ol = _dot(p, v_s[...])
        return ol[:, :DIFF_DV] / ol[:, DIFF_DV:DIFF_DV + 1]

    for r0 in range(0, q_ref.shape[1], rsub):
        q = q_ref[0, r0:r0 + rsub, :]
        o = attend(q[:, :DIFF_DK], k1_s) - lam * attend(q[:, DIFF_DK:], k2_s)
        o_ref[0, r0:r0 + rsub, :] = (_rms(o, sub_ref[...]) * (1.0 - lam_init)).astype(BF16)


def _attention(q, kv_srcs, lams, subln, lam_init, tq):
    b, lq, _ = q.shape
    n_src = len(kv_srcs)
    lk = sum(k.shape[1] for k, _ in kv_srcs)
    in_specs = [pl.BlockSpec((1, tq, DIFF_DV), lambda bb, h, i: (bb, i, h))]
    args = [q]
    for k, v in kv_srcs:
        spec = pl.BlockSpec((1, k.shape[1], DIFF_DV), lambda bb, h, i: (bb, 0, h))
        in_specs += [spec, spec]
        args += [k, v]
    in_specs += [pl.BlockSpec((1, DIFF_DK), lambda bb, h, i: (0, 0))] * 4
    in_specs += [pl.BlockSpec((1, DIFF_DV), lambda bb, h, i: (0, 0))]
    args += list(lams) + [subln]
    return pl.pallas_call(
        functools.partial(_attn_kernel, n_src=n_src, lam_init=lam_init, rsub=min(ATTN_RSUB, tq)),
        out_shape=jax.ShapeDtypeStruct((b, lq, DIFF_WIDTH), BF16),
        grid=(b, DIFF_HEADS, lq // tq),
        in_specs=in_specs,
        out_specs=pl.BlockSpec((1, tq, DIFF_DV), lambda bb, h, i: (bb, i, h)),
        scratch_shapes=[pltpu.VMEM((lk, DIFF_DK), BF16), pltpu.VMEM((lk, DIFF_DK), BF16),
                        pltpu.VMEM((lk, 2 * DIFF_DV), BF16)],
        compiler_params=_params(("parallel", "parallel", "arbitrary")),
        name="diff_attn_%d" % n_src,
    )(*args)


def _dft_kernel(c_ref, s_ref, g1_ref, g2_ref, w_ref, o_ref):
    z = _dot(c_ref[...], g1_ref[...]) - _dot(s_ref[...], g2_ref[...])
    for j in range(o_ref.shape[0]):
        zj = z[:, j * FNET_WIDTH:(j + 1) * FNET_WIDTH].astype(BF16)
        o_ref[j] = _dot(zj, w_ref[...]).astype(BF16)


def _fnet_dft(cosm, sinm, g1, g2, w_bf, b, tm, nb):
    l = cosm.shape[0]
    tn = nb * FNET_WIDTH
    return pl.pallas_call(
        _dft_kernel,
        out_shape=jax.ShapeDtypeStruct((b, l, FNET_WIDTH), BF16),
        grid=(l // tm, b // nb),
        in_specs=[pl.BlockSpec((tm, l), lambda i, j: (i, 0)),
                  pl.BlockSpec((tm, l), lambda i, j: (i, 0)),
                  pl.BlockSpec((l, tn), lambda i, j: (0, j)),
                  pl.BlockSpec((l, tn), lambda i, j: (0, j)),
                  pl.BlockSpec((FNET_WIDTH, FNET_WIDTH), lambda i, j: (0, 0))],
        out_specs=pl.BlockSpec((nb, tm, FNET_WIDTH), lambda i, j: (j, i, 0)),
        compiler_params=_params(("parallel", "parallel")),
        name="fnet_dft",
    )(cosm, sinm, g1, g2, w_bf)


def _s5_kernel(uf_ref, ur_ref, perm_ref, permt_ref, bf_ref, br_ref, lf_ref, lr_ref, cf_ref, cr_ref,
               h0f_ref, h0r_ref, yf_ref, yr_ref, hef_ref, her_ref, hs_f, hs_r, hb_f, hb_r,
               *, tc, nb):
    j = pl.program_id(0)

    @pl.when(j == 0)
    def _():
        hs_f[...] = h0f_ref[...]
        hs_r[...] = h0r_ref[...]

    def drive(u_ref, b_ref):
        u_bt = jnp.concatenate(
            [u_ref[:, bb * S5_WIDTH:(bb + 1) * S5_WIDTH] for bb in range(nb)], axis=0).astype(BF16)
        u_tb = _dot(perm_ref[...], u_bt).astype(BF16)
        return _dot(u_tb, b_ref[...])

    def scan(drv, l_ref, hs, hb, reverse):
        lre = l_ref[:, :S5_LANES]
        lim = l_ref[:, S5_LANES:]
        hr, hi = hs[:, :S5_LANES], hs[:, S5_LANES:]
        for t in (range(tc - 1, -1, -1) if reverse else range(tc)):
            d = drv[t * nb:(t + 1) * nb, :]
            hr, hi = (lre * hr - lim * hi + d[:, :S5_LANES], lre * hi + lim * hr + d[:, S5_LANES:])
            hb[t * nb:(t + 1) * nb, :S5_LANES] = hr.astype(BF16)
            hb[t * nb:(t + 1) * nb, S5_LANES:] = hi.astype(BF16)
        hs[:, :S5_LANES] = hr
        hs[:, S5_LANES:] = hi

    def readout(hb, c_ref, y_ref):
        y = _dot(hb[...], c_ref[...])
        y_hi = y.astype(BF16)
        y_lo = (y - y_hi.astype(F32)).astype(BF16)
        y_bt = _dot(permt_ref[...], y_hi) + _dot(permt_ref[...], y_lo)
        for bb in range(nb):
            y_ref[:, bb * S5_WIDTH:(bb + 1) * S5_WIDTH] = y_bt[bb * tc:(bb + 1) * tc, :]

    drv_f = drive(uf_ref, bf_ref)
    drv_r = drive(ur_ref, br_ref)
    scan(drv_f, lf_ref, hs_f, hb_f, False)
    scan(drv_r, lr_ref, hs_r, hb_r, True)
    readout(hb_f, cf_ref, yf_ref)
    readout(hb_r, cr_ref, yr_ref)

    @pl.when(j == pl.num_programs(0) - 1)
    def _():
        hef_ref[...] = hs_f[...]
        her_ref[...] = hs_r[...]


def _s5_scan(u, mats, h0f, h0r, nb, tc):
    l = u.shape[0]
    n = l // tc
    r = tc * nb
    bmf, bmr, lf, lr, cmf, cmr = mats
    rows = jnp.arange(r, dtype=jnp.int32)
    perm = (rows[None, :] == ((rows % nb) * tc + rows // nb)[:, None]).astype(BF16)
    permt = perm.T
    full = lambda a: pl.BlockSpec(a.shape, lambda j: (0,) * a.ndim)
    fwd = pl.BlockSpec((tc, nb * S5_WIDTH), lambda j: (j, 0))
    rev = pl.BlockSpec((tc, nb * S5_WIDTH), lambda j: (n - 1 - j, 0))
    st = jax.ShapeDtypeStruct((nb, 2 * S5_LANES), F32)
    return pl.pallas_call(
        functools.partial(_s5_kernel, tc=tc, nb=nb),
        out_shape=(jax.ShapeDtypeStruct(u.shape, F32), jax.ShapeDtypeStruct(u.shape, F32), st, st),
        grid=(n,),
        in_specs=[fwd, rev, full(perm), full(permt), full(bmf), full(bmr), full(lf), full(lr),
                  full(cmf), full(cmr), full(h0f), full(h0r)],
        out_specs=(fwd, rev, full(h0f), full(h0r)),
        scratch_shapes=[pltpu.VMEM((nb, 2 * S5_LANES), F32), pltpu.VMEM((nb, 2 * S5_LANES), F32),
                        pltpu.VMEM((r, 2 * S5_LANES), BF16), pltpu.VMEM((r, 2 * S5_LANES), BF16)],
        compiler_params=_params(("arbitrary",)),
        name="s5_scan",
    )(u, u, perm, permt, bmf, bmr, lf, lr, cmf, cmr, h0f, h0r)


def _s5_matrices(a_re, a_im, log_dt, b_re, b_im, c_re, c_im):
    dt = jnp.exp(log_dt)[:, None]
    mag = jnp.exp(a_re * dt)
    lr, li = mag * jnp.cos(a_im * dt), mag * jnp.sin(a_im * dt)
    nr, ni = lr - 1.0, li
    den = a_re * a_re + a_im * a_im
    cr = (nr * a_re + ni * a_im) / den
    ci = (ni * a_re - nr * a_im) / den
    bbr = cr[..., None] * b_re - ci[..., None] * b_im
    bbi = cr[..., None] * b_im + ci[..., None] * b_re
    eye = jnp.eye(S5_GROUPS, dtype=F32)

    def drive_mat(bb):
        return jnp.einsum('gpc,gh->gchp', bb, eye).reshape(S5_WIDTH, S5_LANES)

    def read_mat(cc):
        return jnp.einsum('gcp,gh->gphc', cc, eye).reshape(S5_LANES, S5_WIDTH)

    bm = jnp.concatenate([drive_mat(bbr), drive_mat(bbi)], axis=1).astype(BF16)
    cm = jnp.concatenate([read_mat(c_re), -read_mat(c_im)], axis=0).astype(BF16)
    lam = jnp.concatenate([lr.reshape(1, S5_LANES), li.reshape(1, S5_LANES)], axis=1)
    return bm, lam, cm


def _outproj_kernel(a_ref, f_ref, yf_ref, yr_ref, u_ref, d_ref, wglu_ref, wa_ref, wf_ref, ws_ref,
                    x_ref, gate_ref, g_ref, o_ref):
    yy = yf_ref[...] + yr_ref[...] + d_ref[...] * u_ref[...]
    yy = jax.nn.gelu(yy)
    s = yy * jax.nn.sigmoid(_dot(yy.astype(BF16), wglu_ref[...]))
    mix = (_dot(a_ref[0], wa_ref[...]) + _dot(f_ref[0], wf_ref[...])
           + _dot(s.astype(BF16), ws_ref[...]))
    o_ref[0] = x_ref[0] + gate_ref[0] * _rms(mix, g_ref[...])


def _outproj(a, fo, yf, yr, u, d, wglu, w_out, x, gate, g, tl):
    b, l, dm = x.shape
    wa, wf, ws = (w_out[:DIFF_WIDTH], w_out[DIFF_WIDTH:DIFF_WIDTH + FNET_WIDTH],
                  w_out[DIFF_WIDTH + FNET_WIDTH:])
    tok = lambda w: pl.BlockSpec((1, tl, w), lambda i, bb: (bb, i, 0))
    tb = pl.BlockSpec((tl, S5_WIDTH), lambda i, bb: (i, bb))
    full = lambda arr: pl.BlockSpec(arr.shape, lambda i, bb: (0,) * arr.ndim)
    return pl.pallas_call(
        _outproj_kernel,
        out_shape=jax.ShapeDtypeStruct((b, l, dm), F32),
        grid=(l // tl, b),
        in_specs=[tok(DIFF_WIDTH), tok(FNET_WIDTH), tb, tb, tb, full(d), full(wglu),
                  full(wa), full(wf), full(ws), tok(dm),
                  pl.BlockSpec((1, 1, dm), lambda i, bb: (bb, 0, 0)), full(g)],
        out_specs=tok(dm),
        compiler_params=_params(("parallel", "parallel")),
        name="outproj",
    )(a, fo, yf, yr, u, d, wglu, wa, wf, ws, x, gate, g)


def _ffn_kernel(x_ref, sc_ref, sh_ref, g_ref, wg_ref, wu_ref, wd_ref, gate_ref, gp_ref,
                o_ref, h_s, acc_s):
    f = pl.program_id(2)

    @pl.when(f == 0)
    def _():
        h = _rms(x_ref[0], g_ref[...]) * (1.0 + sc_ref[0]) + sh_ref[0]
        h_s[...] = h.astype(BF16)
        acc_s[...] = jnp.zeros_like(acc_s)

    hb = h_s[...]
    gt = _dot(hb, wg_ref[...])
    up = _dot(hb, wu_ref[...])
    act = (gt * jax.nn.sigmoid(gt) * up).astype(BF16)
    acc_s[...] += _dot(act, wd_ref[...])

    @pl.when(f == pl.num_programs(2) - 1)
    def _():
        o_ref[0] = x_ref[0] + gate_ref[0] * _rms(acc_s[...], gp_ref[...])


def _ffn(x, sc, sh, g, wg, wu, wd, gate, gp, tm, tf):
    b, l, d = x.shape
    fdim = wg.shape[1]
    tok = pl.BlockSpec((1, tm, d), lambda bb, i, f: (bb, i, 0))
    modv = pl.BlockSpec((1, 1, d), lambda bb, i, f: (bb, 0, 0))
    vec = pl.BlockSpec((1, d), lambda bb, i, f: (0, 0))
    return pl.pallas_call(
        _ffn_kernel,
        out_shape=jax.ShapeDtypeStruct((b, l, d), F32),
        grid=(b, l // tm, fdim // tf),
        in_specs=[tok, modv, modv, vec,
                  pl.BlockSpec((d, tf), lambda bb, i, f: (0, f)),
                  pl.BlockSpec((d, tf), lambda bb, i, f: (0, f)),
                  pl.BlockSpec((tf, d), lambda bb, i, f: (f, 0)),
                  modv, vec],
        out_specs=tok,
        scratch_shapes=[pltpu.VMEM((tm, d), BF16), pltpu.VMEM((tm, d), F32)],
        compiler_params=_params(("parallel", "parallel", "arbitrary")),
        name="dense_ffn",
    )(x, sc, sh, g, wg, wu, wd, gate, gp)


def _router_kernel(x_ref, sc_ref, sh_ref, g_ref, r_ref, h_ref, info_ref):
    h = _rms(x_ref[0], g_ref[...]) * (1.0 + sc_ref[0]) + sh_ref[0]
    h_ref[0] = h
    hi = h.astype(BF16)
    lo = (h - hi.astype(F32)).astype(BF16)
    logits = _dot(hi, r_ref[0]) + _dot(hi, r_ref[1]) + _dot(lo, r_ref[0])
    lane = lax.broadcasted_iota(jnp.int32, logits.shape, 1).astype(F32)
    neg = jnp.float32(-jnp.inf)
    logits = jnp.where(lane < N_EXPERTS, logits, neg)
    m1 = jnp.max(logits, axis=-1, keepdims=True)
    i1 = jnp.min(jnp.where(logits == m1, lane, float(LANES)), axis=-1, keepdims=True)
    rest = jnp.where(lane == i1, neg, logits)
    m2 = jnp.max(rest, axis=-1, keepdims=True)
    i2 = jnp.min(jnp.where(rest == m2, lane, float(LANES)), axis=-1, keepdims=True)
    e2 = jnp.exp(m2 - m1)
    w1 = 1.0 / (1.0 + e2)
    info_ref[0] = jnp.where(lane == 0, i1, jnp.where(lane == 1, i2, jnp.where(
        lane == 2, w1, jnp.where(lane == 3, e2 * w1, 0.0))))


def _router(x, sc, sh, g, router, tm):
    b, l, d = x.shape
    tok = pl.BlockSpec((1, tm, d), lambda bb, i: (bb, i, 0))
    modv = pl.BlockSpec((1, 1, d), lambda bb, i: (bb, 0, 0))
    return pl.pallas_call(
        _router_kernel,
        out_shape=(jax.ShapeDtypeStruct((b, l, d), F32), jax.ShapeDtypeStruct((b, l, LANES), F32)),
        grid=(b, l // tm),
        in_specs=[tok, modv, modv, pl.BlockSpec((1, d), lambda bb, i: (0, 0)),
                  pl.BlockSpec(router.shape, lambda bb, i: (0, 0, 0))],
        out_specs=(tok, pl.BlockSpec((1, tm, LANES), lambda bb, i: (bb, i, 0))),
        compiler_params=_params(("parallel", "parallel")),
        name="moe_router",
    )(x, sc, sh, g, router)


def _routing_tables(ids, wts, tm, n_tiles):
    n2 = ids.size
    e = ids.reshape(n2)
    onehot = (e[:, None] == jnp.arange(N_EXPERTS, dtype=jnp.int32)[None, :]).astype(jnp.int32)
    csum = jnp.cumsum(onehot, axis=0)
    rank = jnp.take_along_axis(csum, e[:, None], axis=1)[:, 0] - 1
    counts = csum[-1]
    padded = ((counts + tm - 1) // tm) * tm
    ends = jnp.cumsum(padded)
    starts = ends - padded
    dest = (starts[e] + rank).astype(jnp.int32)
    tile_row0 = jnp.arange(n_tiles, dtype=jnp.int32) * tm
    tile_e = jnp.minimum(jnp.sum((ends[None, :] <= tile_row0[:, None]).astype(jnp.int32), axis=1),
                         N_EXPERTS - 1).astype(jnp.int32)
    rows = n_tiles * tm
    pair = jnp.full((rows,), n2, jnp.int32).at[dest].set(jnp.arange(n2, dtype=jnp.int32))
    src = jnp.minimum(pair // 2, n2 // 2 - 1)
    wsort = jnp.concatenate([wts.reshape(n2), jnp.zeros((1,), F32)])[pair]
    n_live_tiles = (ends[-1] // tm).astype(jnp.int32)
    return src, wsort, tile_e, n_live_tiles, dest


def _gffn_kernel(te_ref, nv_ref, cur_ref, nxt_ref, h_hbm, wg_ref, wu_ref, wd_ref, ws_ref,
                 o_ref, xbuf, hb_s, acc_s, sem, *, tm, nf):
    t = pl.program_id(0)
    f = pl.program_id(1)
    n_live = nv_ref[0]
    slot = t % 2
    chunk = tm // nf

    def row_copy(idx_ref, r, dst_slot):
        return pltpu.make_async_copy(h_hbm.at[pl.ds(idx_ref[0, 0, r], 1)],
                                     xbuf.at[dst_slot, pl.ds(r, 1)], sem.at[dst_slot])

    def wait_tile(dst_slot):
        pltpu.make_async_copy(h_hbm.at[pl.ds(0, tm)], xbuf.at[dst_slot], sem.at[dst_slot]).wait()

    @pl.when((t == 0) & (f == 0))
    def _():
        def body(r, c):
            row_copy(cur_ref, r, 0).start()
            return c
        lax.fori_loop(0, tm, body, 0)

    @pl.when(t < n_live)
    def _():
        @pl.when(f == 0)
        def _():
            wait_tile(slot)
            hb_s[...] = xbuf[slot].astype(BF16)
            acc_s[...] = jnp.zeros_like(acc_s)

        for j in range(chunk):
            row_copy(nxt_ref, f * chunk + j, 1 - slot).start()

        hb = hb_s[...]
        gt = _dot(hb, wg_ref[0])
        up = _dot(hb, wu_ref[0])
        act = (gt * jax.nn.sigmoid(gt) * up).astype(BF16)
        acc_s[...] += _dot(act, wd_ref[0])

        @pl.when(f == nf - 1)
        def _():
            o_ref[...] = acc_s[...] * ws_ref[...]

        @pl.when((f == nf - 1) & (t == n_live - 1))
        def _():
            wait_tile(1 - slot)

    @pl.when((t >= n_live) & (f == nf - 1))
    def _():
        o_ref[...] = jnp.zeros_like(o_ref)


def _grouped_ffn(h2d, src, wsort, tile_e, n_live, wg, wu, wd, tm, tf):
    n, d = h2d.shape
    n_tiles = tile_e.shape[0]
    fdim = wg.shape[2]
    nf = fdim // tf
    src3 = src.reshape(n_tiles, 1, tm)
    smem_blk = lambda off: pl.BlockSpec(
        (1, 1, tm), lambda t, f, te, nv: (jnp.minimum(t + off, n_tiles - 1), 0, 0),
        memory_space=pltpu.SMEM)
    grid_spec = pltpu.PrefetchScalarGridSpec(
        num_scalar_prefetch=2,
        grid=(n_tiles, nf),
        in_specs=[smem_blk(0), smem_blk(1),
                  pl.BlockSpec(memory_space=pl.ANY),
                  pl.BlockSpec((1, d, tf), lambda t, f, te, nv: (te[t], 0, f)),
                  pl.BlockSpec((1, d, tf), lambda t, f, te, nv: (te[t], 0, f)),
                  pl.BlockSpec((1, tf, d), lambda t, f, te, nv: (te[t], f, 0)),
                  pl.BlockSpec((tm, 1), lambda t, f, te, nv: (t, 0))],
        out_specs=pl.BlockSpec((tm, d), lambda t, f, te, nv: (t, 0)),
        scratch_shapes=[pltpu.VMEM((2, tm, d), F32), pltpu.VMEM((tm, d), BF16),
                        pltpu.VMEM((tm, d), F32), pltpu.SemaphoreType.DMA((2,))])
    return pl.pallas_call(
        functools.partial(_gffn_kernel, tm=tm, nf=nf),
        out_shape=jax.ShapeDtypeStruct((n_tiles * tm, d), F32),
        grid_spec=grid_spec,
        compiler_params=_params(("arbitrary", "arbitrary")),
        name="moe_grouped_ffn",
    )(tile_e, n_live.reshape(1), src3, src3, h2d, wg, wu, wd, wsort.reshape(n_tiles * tm, 1))


def _combine_kernel(d0_ref, d1_ref, ys_hbm, x_ref, gate_ref, gp_ref, o_ref, b0, b1, sem, *, tc):
    def body(r, c):
        pltpu.make_async_copy(ys_hbm.at[pl.ds(d0_ref[0, 0, r], 1)], b0.at[pl.ds(r, 1)], sem.at[0]).start()
        pltpu.make_async_copy(ys_hbm.at[pl.ds(d1_ref[0, 0, r], 1)], b1.at[pl.ds(r, 1)], sem.at[1]).start()
        return c
    lax.fori_loop(0, tc, body, 0)
    pltpu.make_async_copy(ys_hbm.at[pl.ds(0, tc)], b0, sem.at[0]).wait()
    pltpu.make_async_copy(ys_hbm.at[pl.ds(0, tc)], b1, sem.at[1]).wait()
    o_ref[0] = x_ref[0] + gate_ref[0] * _rms(b0[...] + b1[...], gp_ref[...])


def _combine(ys, dest, x, gate, gp, tc):
    b, l, d = x.shape
    nt = l // tc
    dd = dest.reshape(b * nt, tc, 2)
    d0 = dd[:, :, 0].reshape(b * nt, 1, tc)
    d1 = dd[:, :, 1].reshape(b * nt, 1, tc)
    smem = pl.BlockSpec((1, 1, tc), lambda bb, i: (bb * nt + i, 0, 0), memory_space=pltpu.SMEM)
    tok = pl.BlockSpec((1, tc, d), lambda bb, i: (bb, i, 0))
    return pl.pallas_call(
        functools.partial(_combine_kernel, tc=tc),
        out_shape=jax.ShapeDtypeStruct((b, l, d), F32),
        grid=(b, nt),
        in_specs=[smem, smem, pl.BlockSpec(memory_space=pl.ANY), tok,
                  pl.BlockSpec((1, 1, d), lambda bb, i: (bb, 0, 0)),
                  pl.BlockSpec((1, d), lambda bb, i: (0, 0))],
        out_specs=tok,
        scratch_shapes=[pltpu.VMEM((tc, d), F32), pltpu.VMEM((tc, d), F32),
                        pltpu.SemaphoreType.DMA((2,))],
        compiler_params=_params(("arbitrary", "arbitrary")),
        name="moe_combine",
    )(d0, d1, ys, x, gate, gp)


def _moe(x, sc, sh, gpre, router, wg, wu, wd, gate, gpost, tm, tf):
    b, l, d = x.shape
    n = b * l
    h, info = _router(x, sc, sh, gpre, router, min(1024, l))
    info = info.reshape(n, LANES)
    ids = info[:, :2].astype(jnp.int32)
    wts = info[:, 2:4]
    n_tiles = (2 * n) // tm + N_EXPERTS
    src, wsort, tile_e, n_live, dest = _routing_tables(ids, wts, tm, n_tiles)
    ys = _grouped_ffn(h.reshape(n, d), src, wsort, tile_e, n_live, wg, wu, wd, tm, tf)
    return _combine(ys, dest, x, gate, gpost, min(512, l))


def _rope_tables(s):
    half = DIFF_DK // 2
    n_freq = half // 2
    inv = ROPE_THETA ** (-jnp.arange(n_freq, dtype=F32) / n_freq)
    t = jnp.arange(s, dtype=jnp.int32)
    rows = (t // GRID_W).astype(F32)[:, None]
    cols = (t % GRID_W).astype(F32)[:, None]
    lane = jnp.arange(LANES, dtype=jnp.int32)
    dd = lane % DIFF_DK
    pos = jnp.where((dd < half)[None, :], rows, cols)
    ang = pos * inv[dd % n_freq][None, :]
    first = ((dd % half) < n_freq)[None, :]
    cos, sin = jnp.cos(ang), jnp.sin(ang)
    return cos, jnp.where(first, -sin, 0.0), jnp.where(first, 0.0, sin)


def _dft_tables(n):
    j = jnp.arange(n, dtype=jnp.int32)
    idx = (j[:, None] * j[None, :]) % n
    ang = idx.astype(F32) * (2.0 * math.pi / n)
    sc = n ** -0.5
    return jnp.cos(ang) * sc, jnp.sin(ang) * sc


def _channel_dft():
    c, s = _dft_tables(FNET_GROUP_CH)
    eye = jnp.eye(FNET_GROUPS, dtype=F32)
    return jnp.concatenate([jnp.kron(eye, c), jnp.kron(eye, s)], axis=1).astype(BF16)


def kernel(x, c, ctx, c_ctx, ada_w, ada_b, norm_mix_pre, norm_mix_post, norm_ffn_pre, norm_ffn_post,
           w_in, w_out, diff_lq1, diff_lk1, diff_lq2, diff_lk2, diff_subln, fnet_w,
           s5_a_re, s5_a_im, s5_log_dt, s5_b_re, s5_b_im, s5_c_re, s5_c_im, s5_d, s5_w_glu,
           ffn_w_gate, ffn_w_up, ffn_w_down, moe_router, moe_w_gate, moe_w_up, moe_w_down):
    b, s, d = x.shape
    lc = ctx.shape[1]
    depth = ada_w.shape[0]

    cc = jnp.zeros((MOD_ROWS, d), F32).at[:b].set(c).at[b].set(c_ctx)
    mod = _modulation(cc, ada_w, ada_b)

    rope_tabs = _rope_tables(s)
    cs64 = _channel_dft()
    dft = {n: tuple(t.astype(BF16) for t in _dft_tables(n)) for n in (s, lc)}

    xc = ctx
    for l in range(depth):
        need_ctx = l < depth - 1
        lam_init = 0.8 - 0.6 * math.exp(-0.3 * l)
        m_lat = [mod[l, :b, i * d:(i + 1) * d].reshape(b, 1, d) for i in range(6)]
        m_ctx = [jnp.broadcast_to(mod[l, b, i * d:(i + 1) * d].reshape(1, 1, d), (b, 1, d))
                 for i in range(6)]
        row = lambda v: v.reshape(1, -1).astype(F32)
        w_in_bf = w_in[l].astype(BF16)
        w_out_bf = w_out[l].astype(BF16)
        fw_bf = fnet_w[l].astype(BF16)
        wglu_bf = s5_w_glu[l].astype(BF16)
        lams = (row(diff_lq1[l]), row(diff_lk1[l]), row(diff_lq2[l]), row(diff_lk2[l]))
        subln = row(diff_subln[l])

        q, k, v, g1, g2, u = _inproj(x, m_lat[1], m_lat[0], row(norm_mix_pre[l]), w_in_bf, cs64,
                                     rope_tabs, min(512, s))
        qc, kc, vc, g1c, g2c, uc = _inproj(xc, m_ctx[1], m_ctx[0], row(norm_mix_pre[l]), w_in_bf,
                                           cs64, None, lc)
        a_lat = _attention(q, [(kc, vc), (k, v)], lams, subln, lam_init, min(512, s))
        f_lat = _fnet_dft(*dft[s], g1, g2, fw_bf, b, min(512, s), 2)

        mats = [_s5_matrices(s5_a_re[l, dr], s5_a_im[l, dr], s5_log_dt[l, dr], s5_b_re[l, dr],
                             s5_b_im[l, dr], s5_c_re[l, dr], s5_c_im[l, dr]) for dr in (0, 1)]
        mats = (mats[0][0], mats[1][0], mats[0][1], mats[1][1], mats[0][2], mats[1][2])
        zero_state = jnp.zeros((b, 2 * S5_LANES), F32)
        ycf, ycr, hcf, hcr = _s5_scan(uc, mats, zero_state, zero_state, b, S5_TC)
        yf, yr, _, _ = _s5_scan(u, mats, hcf, hcr, b, S5_TC)

        x = _outproj(a_lat, f_lat, yf, yr, u, row(s5_d[l]), wglu_bf, w_out_bf,
                     x, m_lat[2], row(norm_mix_post[l]), min(512, s))
        if need_ctx:
            a_ctx = _attention(qc, [(kc, vc)], lams, subln, lam_init, lc)
            f_ctx = _fnet_dft(*dft[lc], g1c, g2c, fw_bf, b, lc, 2)
            xc = _outproj(a_ctx, f_ctx, ycf, ycr, uc, row(s5_d[l]), wglu_bf,
                          w_out_bf, xc, m_ctx[2], row(norm_mix_post[l]), lc)

        i = l // 2
        gpre, gpost = row(norm_ffn_pre[l]), row(norm_ffn_post[l])
        xc1 = xc.reshape(1, b * lc, d)
        if l % 2 == 0:
            wg, wu, wd = (ffn_w_gate[i].astype(BF16), ffn_w_up[i].astype(BF16),
                          ffn_w_down[i].astype(BF16))
            tf = FFN_DENSE_TF
            x = _ffn(x, m_lat[4], m_lat[3], gpre, wg, wu, wd, m_lat[5], gpost, min(1024, s), tf)
            if need_ctx:
                xc1 = _ffn(xc1, m_ctx[4][:1], m_ctx[3][:1], gpre, wg, wu, wd, m_ctx[5][:1], gpost,
                           min(1024, b * lc), tf)
        else:
            wg, wu, wd = (moe_w_gate[i].astype(BF16), moe_w_up[i].astype(BF16),
                          moe_w_down[i].astype(BF16))
            r = jnp.zeros((d, LANES), F32).at[:, :N_EXPERTS].set(moe_router[i])
            r_hi = r.astype(BF16)
            router = jnp.stack([r_hi, (r - r_hi.astype(F32)).astype(BF16)])
            x = _moe(x, m_lat[4], m_lat[3], gpre, router, wg, wu, wd, m_lat[5], gpost,
                     MOE_TM, MOE_TF)
            if need_ctx:
                xc1 = _moe(xc1, m_ctx[4][:1], m_ctx[3][:1], gpre, router, wg, wu, wd, m_ctx[5][:1],
                           gpost, MOE_TM, MOE_TF)
        xc = xc1.reshape(b, lc, d)
    return x
```

```python
import functools
import math

import jax
import jax.numpy as jnp
from jax import lax
from jax.experimental import pallas as pl
from jax.experimental.pallas import tpu as pltpu

F32 = jnp.float32
BF16 = jnp.bfloat16

D_MODEL = 1024
DEPTH = 2
GRID_W = 64
EPS = 1e-6
DIFF_HEADS = 4
DIFF_DK = 64
DIFF_DV = 2 * DIFF_DK
DIFF_WIDTH = DIFF_HEADS * DIFF_DV
DIFF_QK_WIDTH = DIFF_HEADS * 2 * DIFF_DK
ROPE_THETA = 10000.0
FNET_GROUPS = 4
FNET_GROUP_CH = 64
FNET_WIDTH = FNET_GROUPS * FNET_GROUP_CH
S5_CH = 16
S5_GROUPS = 16
S5_STATE = 64
S5_WIDTH = S5_GROUPS * S5_CH
S5_LANES = S5_GROUPS * S5_STATE
IN_WIDTH = 2 * DIFF_QK_WIDTH + DIFF_WIDTH + FNET_WIDTH + S5_WIDTH
N_EXPERTS = 8
LANES = 128
LOG2E = math.log2(math.e)
MOD_ROWS = 24
FFN_DENSE_TF = 1408
ATTN_RSUB = 128
S5_TC = 32
MOE_TB = 1024
MOE_SEG = 32
MOE_TM = 1024
MOE_TF = 896

VMEM_LIMIT = 56 * 1024 * 1024


def _params(sem):
    return pltpu.CompilerParams(dimension_semantics=sem, vmem_limit_bytes=VMEM_LIMIT)


def _rms(x, g):
    return x * lax.rsqrt(jnp.mean(x * x, axis=-1, keepdims=True) + EPS) * g


def _dot(a, b):
    return jnp.dot(a, b, preferred_element_type=F32)


def _mod_kernel(c_ref, w_ref, b_ref, o_ref):
    c = c_ref[...]
    sc = c * jax.nn.sigmoid(c)
    o_ref[0] = jnp.dot(sc, w_ref[0], preferred_element_type=F32,
                       precision=lax.Precision.HIGHEST) + b_ref[0]


def _modulation(cc, ada_w, ada_b):
    depth, d, n = ada_w.shape
    tn = 1536
    return pl.pallas_call(
        _mod_kernel,
        out_shape=jax.ShapeDtypeStruct((depth, MOD_ROWS, n), F32),
        grid=(depth, n // tn),
        in_specs=[pl.BlockSpec((MOD_ROWS, d), lambda l, j: (0, 0)),
                  pl.BlockSpec((1, d, tn), lambda l, j: (l, 0, j)),
                  pl.BlockSpec((1, 1, tn), lambda l, j: (l, 0, j))],
        out_specs=pl.BlockSpec((1, MOD_ROWS, tn), lambda l, j: (l, 0, j)),
        compiler_params=_params(("parallel", "parallel")),
        name="adaln_mod",
    )(cc, ada_w, ada_b.reshape(depth, 1, n))


def _inproj_kernel(*refs, rope):
    if rope:
        (x_ref, sc_ref, sh_ref, g_ref, w_ref, cs_ref, cos_ref, sina_ref, sinb_ref,
         q_ref, k_ref, v_ref, g1_ref, g2_ref, u_ref) = refs
    else:
        (x_ref, sc_ref, sh_ref, g_ref, w_ref, cs_ref,
         q_ref, k_ref, v_ref, g1_ref, g2_ref, u_ref) = refs
    h = _rms(x_ref[0], g_ref[...]) * (1.0 + sc_ref[0]) + sh_ref[0]
    hb = h.astype(BF16)

    def proj(lo, hi):
        return _dot(hb, w_ref[:, lo:hi])

    def rotate(t):
        outs = []
        for j in range(t.shape[1] // LANES):
            tb = t[:, j * LANES:(j + 1) * LANES]
            outs.append(tb * cos_ref[...]
                        + pltpu.roll(tb, LANES - 16, axis=1) * sina_ref[...]
                        + pltpu.roll(tb, 16, axis=1) * sinb_ref[...])
        return jnp.concatenate(outs, axis=1)

    q = proj(0, DIFF_QK_WIDTH)
    k = proj(DIFF_QK_WIDTH, 2 * DIFF_QK_WIDTH)
    if rope:
        q = rotate(q)
        k = rotate(k)
    q_ref[0] = (q * (DIFF_DK ** -0.5 * LOG2E)).astype(BF16)
    k_ref[0] = k.astype(BF16)
    o = 2 * DIFF_QK_WIDTH
    v_ref[0] = proj(o, o + DIFF_WIDTH).astype(BF16)
    o += DIFF_WIDTH
    f = proj(o, o + FNET_WIDTH).astype(BF16)
    g12 = _dot(f, cs_ref[...])
    g1_ref[...] = g12[:, :FNET_WIDTH].astype(BF16)
    g2_ref[...] = g12[:, FNET_WIDTH:].astype(BF16)
    u_ref[...] = proj(o + FNET_WIDTH, IN_WIDTH)


def _inproj(x, sc, sh, g, w_bf, cs64, rope_tabs, tl):
    b, l, d = x.shape
    rope = rope_tabs is not None
    in_specs = [pl.BlockSpec((1, tl, d), lambda i, bb: (bb, i, 0)),
                pl.BlockSpec((1, 1, d), lambda i, bb: (bb, 0, 0)),
                pl.BlockSpec((1, 1, d), lambda i, bb: (bb, 0, 0)),
                pl.BlockSpec((1, d), lambda i, bb: (0, 0)),
                pl.BlockSpec((d, IN_WIDTH), lambda i, bb: (0, 0)),
                pl.BlockSpec((FNET_WIDTH, 2 * FNET_WIDTH), lambda i, bb: (0, 0))]
    args = [x, sc, sh, g, w_bf, cs64]
    if rope:
        in_specs += [pl.BlockSpec((tl, LANES), lambda i, bb: (i, 0))] * 3
        args += list(rope_tabs)
    tok = lambda w: pl.BlockSpec((1, tl, w), lambda i, bb: (bb, i, 0))
    tb = pl.BlockSpec((tl, FNET_WIDTH), lambda i, bb: (i, bb))
    return pl.pallas_call(
        functools.partial(_inproj_kernel, rope=rope),
        out_shape=(jax.ShapeDtypeStruct((b, l, DIFF_QK_WIDTH), BF16),
                   jax.ShapeDtypeStruct((b, l, DIFF_QK_WIDTH), BF16),
                   jax.ShapeDtypeStruct((b, l, DIFF_WIDTH), BF16),
                   jax.ShapeDtypeStruct((l, b * FNET_WIDTH), BF16),
                   jax.ShapeDtypeStruct((l, b * FNET_WIDTH), BF16),
                   jax.ShapeDtypeStruct((l, b * S5_WIDTH), F32)),
        grid=(l // tl, b),
        in_specs=in_specs,
        out_specs=(tok(DIFF_QK_WIDTH), tok(DIFF_QK_WIDTH), tok(DIFF_WIDTH), tb, tb, tb),
        compiler_params=_params(("parallel", "parallel")),
        name="inproj_rope" if rope else "inproj",
    )(*args)


def _attn_kernel(*refs, n_src, lam_init, rsub):
    q_ref = refs[0]
    kv = refs[1:1 + 2 * n_src]
    lq1, lk1, lq2, lk2, sub_ref, o_ref, k1_s, k2_s, v_s = refs[1 + 2 * n_src:]

    @pl.when(pl.program_id(2) == 0)
    def _():
        off = 0
        for s in range(n_src):
            kk = kv[2 * s][0]
            n = kk.shape[0]
            k1_s[off:off + n, :] = kk[:, :DIFF_DK]
            k2_s[off:off + n, :] = kk[:, DIFF_DK:]
            v_s[off:off + n, :DIFF_DV] = kv[2 * s + 1][0]
            off += n
        v_s[:, DIFF_DV:] = jnp.ones((v_s.shape[0], DIFF_DV), BF16)

    lam = (jnp.exp(jnp.sum(lq1[...] * lk1[...], axis=-1, keepdims=True))
           - jnp.exp(jnp.sum(lq2[...] * lk2[...], axis=-1, keepdims=True)) + lam_init)
    def attend(qj, k_s):
        s = lax.dot_general(qj, k_s[...], (((1,), (1,)), ((), ())), preferred_element_type=F32)
        p = jnp.exp2(s - jnp.max(s, axis=-1, keepdims=True)).astype(BF16)
        ol = _dot(p, v_s[...])
        return ol[:, :DIFF_DV] / ol[:, DIFF_DV:DIFF_DV + 1]

    for r0 in range(0, q_ref.shape[1], rsub):
        q = q_ref[0, r0:r0 + rsub, :]
        o = attend(q[:, :DIFF_DK], k1_s) - lam * attend(q[:, DIFF_DK:], k2_s)
        o_ref[0, r0:r0 + rsub, :] = (_rms(o, sub_ref[...]) * (1.0 - lam_init)).astype(BF16)


def _attention(q, kv_srcs, lams, subln, lam_init, tq):
    b, lq, _ = q.shape
    n_src = len(kv_srcs)
    lk = sum(k.shape[1] for k, _ in kv_srcs)
    in_specs = [pl.BlockSpec((1, tq, DIFF_DV), lambda bb, h, i: (bb, i, h))]
    args = [q]
    for k, v in kv_srcs:
        spec = pl.BlockSpec((1, k.shape[1], DIFF_DV), lambda bb, h, i: (bb, 0, h))
        in_specs += [spec, spec]
        args += [k, v]
    in_specs += [pl.BlockSpec((1, DIFF_DK), lambda bb, h, i: (0, 0))] * 4
    in_specs += [pl.BlockSpec((1, DIFF_DV), lambda bb, h, i: (0, 0))]
    args += list(lams) + [subln]
    return pl.pallas_call(
        functools.partial(_attn_kernel, n_src=n_src, lam_init=lam_init, rsub=min(ATTN_RSUB, tq)),
        out_shape=jax.ShapeDtypeStruct((b, lq, DIFF_WIDTH), BF16),
        grid=(b, DIFF_HEADS, lq // tq),
        in_specs=in_specs,
        out_specs=pl.BlockSpec((1, tq, DIFF_DV), lambda bb, h, i: (bb, i, h)),
        scratch_shapes=[pltpu.VMEM((lk, DIFF_DK), BF16), pltpu.VMEM((lk, DIFF_DK), BF16),
                        pltpu.VMEM((lk, 2 * DIFF_DV), BF16)],
        compiler_params=_params(("parallel", "parallel", "arbitrary")),
        name="diff_attn_%d" % n_src,
    )(*args)


def _dft_kernel(c_ref, s_ref, g1_ref, g2_ref, w_ref, o_ref):
    z = _dot(c_ref[...], g1_ref[...]) - _dot(s_ref[...], g2_ref[...])
    for j in range(o_ref.shape[0]):
        zj = z[:, j * FNET_WIDTH:(j + 1) * FNET_WIDTH].astype(BF16)
        o_ref[j] = _dot(zj, w_ref[...]).astype(BF16)


def _fnet_dft(cosm, sinm, g1, g2, w_bf, b, tm, nb):
    l = cosm.shape[0]
    tn = nb * FNET_WIDTH
    return pl.pallas_call(
        _dft_kernel,
        out_shape=jax.ShapeDtypeStruct((b, l, FNET_WIDTH), BF16),
        grid=(l // tm, b // nb),
        in_specs=[pl.BlockSpec((tm, l), lambda i, j: (i, 0)),
                  pl.BlockSpec((tm, l), lambda i, j: (i, 0)),
                  pl.BlockSpec((l, tn), lambda i, j: (0, j)),
                  pl.BlockSpec((l, tn), lambda i, j: (0, j)),
                  pl.BlockSpec((FNET_WIDTH, FNET_WIDTH), lambda i, j: (0, 0))],
        out_specs=pl.BlockSpec((nb, tm, FNET_WIDTH), lambda i, j: (j, i, 0)),
        compiler_params=_params(("parallel", "parallel")),
        name="fnet_dft",
    )(cosm, sinm, g1, g2, w_bf)


def _s5_kernel(uf_ref, ur_ref, perm_ref, permt_ref, bf_ref, br_ref, lf_ref, lr_ref, cf_ref, cr_ref,
               h0f_ref, h0r_ref, yf_ref, yr_ref, hef_ref, her_ref, hs_f, hs_r, hb_f, hb_r,
               *, tc, nb):
    j = pl.program_id(0)

    @pl.when(j == 0)
    def _():
        hs_f[...] = h0f_ref[...]
        hs_r[...] = h0r_ref[...]

    def drive(u_ref, b_ref):
        u_bt = jnp.concatenate(
            [u_ref[:, bb * S5_WIDTH:(bb + 1) * S5_WIDTH] for bb in range(nb)], axis=0).astype(BF16)
        u_tb = _dot(perm_ref[...], u_bt).astype(BF16)
        return _dot(u_tb, b_ref[...])

    def scan(drv, l_ref, hs, hb, reverse):
        lre = l_ref[:, :S5_LANES]
        lim = l_ref[:, S5_LANES:]
        hr, hi = hs[:, :S5_LANES], hs[:, S5_LANES:]
        for t in (range(tc - 1, -1, -1) if reverse else range(tc)):
            d = drv[t * nb:(t + 1) * nb, :]
            hr, hi = (lre * hr - lim * hi + d[:, :S5_LANES], lre * hi + lim * hr + d[:, S5_LANES:])
            hb[t * nb:(t + 1) * nb, :S5_LANES] = hr.astype(BF16)
            hb[t * nb:(t + 1) * nb, S5_LANES:] = hi.astype(BF16)
        hs[:, :S5_LANES] = hr
        hs[:, S5_LANES:] = hi

    def readout(hb, c_ref, y_ref):
        y = _dot(hb[...], c_ref[...])
        y_hi = y.astype(BF16)
        y_lo = (y - y_hi.astype(F32)).astype(BF16)
        y_bt = _dot(permt_ref[...], y_hi) + _dot(permt_ref[...], y_lo)
        for bb in range(nb):
            y_ref[:, bb * S5_WIDTH:(bb + 1) * S5_WIDTH] = y_bt[bb * tc:(bb + 1) * tc, :]

    drv_f = drive(uf_ref, bf_ref)
    drv_r = drive(ur_ref, br_ref)
    scan(drv_f, lf_ref, hs_f, hb_f, False)
    scan(drv_r, lr_ref, hs_r, hb_r, True)
    readout(hb_f, cf_ref, yf_ref)
    readout(hb_r, cr_ref, yr_ref)

    @pl.when(j == pl.num_programs(0) - 1)
    def _():
        hef_ref[...] = hs_f[...]
        her_ref[...] = hs_r[...]


def _s5_scan(u, mats, h0f, h0r, nb, tc):
    l = u.shape[0]
    n = l // tc
    r = tc * nb
    bmf, bmr, lf, lr, cmf, cmr = mats
    rows = jnp.arange(r, dtype=jnp.int32)
    perm = (rows[None, :] == ((rows % nb) * tc + rows // nb)[:, None]).astype(BF16)
    permt = perm.T
    full = lambda a: pl.BlockSpec(a.shape, lambda j: (0,) * a.ndim)
    fwd = pl.BlockSpec((tc, nb * S5_WIDTH), lambda j: (j, 0))
    rev = pl.BlockSpec((tc, nb * S5_WIDTH), lambda j: (n - 1 - j, 0))
    st = jax.ShapeDtypeStruct((nb, 2 * S5_LANES), F32)
    return pl.pallas_call(
        functools.partial(_s5_kernel, tc=tc, nb=nb),
        out_shape=(jax.ShapeDtypeStruct(u.shape, F32), jax.ShapeDtypeStruct(u.shape, F32), st, st),
        grid=(n,),
        in_specs=[fwd, rev, full(perm), full(permt), full(bmf), full(bmr), full(lf), full(lr),
                  full(cmf), full(cmr), full(h0f), full(h0r)],
        out_specs=(fwd, rev, full(h0f), full(h0r)),
        scratch_shapes=[pltpu.VMEM((nb, 2 * S5_LANES), F32), pltpu.VMEM((nb, 2 * S5_LANES), F32),
                        pltpu.VMEM((r, 2 * S5_LANES), BF16), pltpu.VMEM((r, 2 * S5_LANES), BF16)],
        compiler_params=_params(("arbitrary",)),
        name="s5_scan",
    )(u, u, perm, permt, bmf, bmr, lf, lr, cmf, cmr, h0f, h0r)


def _s5_matrices(a_re, a_im, log_dt, b_re, b_im, c_re, c_im):
    dt = jnp.exp(log_dt)[:, None]
    mag = jnp.exp(a_re * dt)
    lr, li = mag * jnp.cos(a_im * dt), mag * jnp.sin(a_im * dt)
    nr, ni = lr - 1.0, li
    den = a_re * a_re + a_im * a_im
    cr = (nr * a_re + ni * a_im) / den
    ci = (ni * a_re - nr * a_im) / den
    bbr = cr[..., None] * b_re - ci[..., None] * b_im
    bbi = cr[..., None] * b_im + ci[..., None] * b_re
    eye = jnp.eye(S5_GROUPS, dtype=F32)

    def drive_mat(bb):
        return jnp.einsum('gpc,gh->gchp', bb, eye).reshape(S5_WIDTH, S5_LANES)

    def read_mat(cc):
        return jnp.einsum('gcp,gh->gphc', cc, eye).reshape(S5_LANES, S5_WIDTH)

    bm = jnp.concatenate([drive_mat(bbr), drive_mat(bbi)], axis=1).astype(BF16)
    cm = jnp.concatenate([read_mat(c_re), -read_mat(c_im)], axis=0).astype(BF16)
    lam = jnp.concatenate([lr.reshape(1, S5_LANES), li.reshape(1, S5_LANES)], axis=1)
    return bm, lam, cm


def _outproj_kernel(a_ref, f_ref, yf_ref, yr_ref, u_ref, d_ref, wglu_ref, wa_ref, wf_ref, ws_ref,
                    x_ref, gate_ref, g_ref, o_ref):
    yy = yf_ref[...] + yr_ref[...] + d_ref[...] * u_ref[...]
    yy = jax.nn.gelu(yy)
    s = yy * jax.nn.sigmoid(_dot(yy.astype(BF16), wglu_ref[...]))
    mix = (_dot(a_ref[0], wa_ref[...]) + _dot(f_ref[0], wf_ref[...])
           + _dot(s.astype(BF16), ws_ref[...]))
    o_ref[0] = x_ref[0] + gate_ref[0] * _rms(mix, g_ref[...])


def _outproj(a, fo, yf, yr, u, d, wglu, w_out, x, gate, g, tl):
    b, l, dm = x.shape
    wa, wf, ws = (w_out[:DIFF_WIDTH], w_out[DIFF_WIDTH:DIFF_WIDTH + FNET_WIDTH],
                  w_out[DIFF_WIDTH + FNET_WIDTH:])
    tok = lambda w: pl.BlockSpec((1, tl, w), lambda i, bb: (bb, i, 0))
    tb = pl.BlockSpec((tl, S5_WIDTH), lambda i, bb: (i, bb))
    full = lambda arr: pl.BlockSpec(arr.shape, lambda i, bb: (0,) * arr.ndim)
    return pl.pallas_call(
        _outproj_kernel,
        out_shape=jax.ShapeDtypeStruct((b, l, dm), F32),
        grid=(l // tl, b),
        in_specs=[tok(DIFF_WIDTH), tok(FNET_WIDTH), tb, tb, tb, full(d), full(wglu),
                  full(wa), full(wf), full(ws), tok(dm),
                  pl.BlockSpec((1, 1, dm), lambda i, bb: (bb, 0, 0)), full(g)],
        out_specs=tok(dm),
        compiler_params=_params(("parallel", "parallel")),
        name="outproj",
    )(a, fo, yf, yr, u, d, wglu, wa, wf, ws, x, gate, g)


def _ffn_kernel(x_ref, sc_ref, sh_ref, g_ref, wg_ref, wu_ref, wd_ref, gate_ref, gp_ref,
                o_ref, h_s, acc_s):
    f = pl.program_id(2)

    @pl.when(f == 0)
    def _():
        h = _rms(x_ref[0], g_ref[...]) * (1.0 + sc_ref[0]) + sh_ref[0]
        h_s[...] = h.astype(BF16)
        acc_s[...] = jnp.zeros_like(acc_s)

    hb = h_s[...]
    gt = _dot(hb, wg_ref[...])
    up = _dot(hb, wu_ref[...])
    act = (gt * jax.nn.sigmoid(gt) * up).astype(BF16)
    acc_s[...] += _dot(act, wd_ref[...])

    @pl.when(f == pl.num_programs(2) - 1)
    def _():
        o_ref[0] = x_ref[0] + gate_ref[0] * _rms(acc_s[...], gp_ref[...])


def _ffn(x, sc, sh, g, wg, wu, wd, gate, gp, tm, tf):
    b, l, d = x.shape
    fdim = wg.shape[1]
    tok = pl.BlockSpec((1, tm, d), lambda bb, i, f: (bb, i, 0))
    modv = pl.BlockSpec((1, 1, d), lambda bb, i, f: (bb, 0, 0))
    vec = pl.BlockSpec((1, d), lambda bb, i, f: (0, 0))
    return pl.pallas_call(
        _ffn_kernel,
        out_shape=jax.ShapeDtypeStruct((b, l, d), F32),
        grid=(b, l // tm, fdim // tf),
        in_specs=[tok, modv, modv, vec,
                  pl.BlockSpec((d, tf), lambda bb, i, f: (0, f)),
                  pl.BlockSpec((d, tf), lambda bb, i, f: (0, f)),
                  pl.BlockSpec((tf, d), lambda bb, i, f: (f, 0)),
                  modv, vec],
        out_specs=tok,
        scratch_shapes=[pltpu.VMEM((tm, d), BF16), pltpu.VMEM((tm, d), F32)],
        compiler_params=_params(("parallel", "parallel", "arbitrary")),
        name="dense_ffn",
    )(x, sc, sh, g, wg, wu, wd, gate, gp)


def _router_kernel(x_ref, sc_ref, sh_ref, g_ref, r_ref, tri_ref, hx_ref, info_ref, infot_ref, cnt_ref):
    d = x_ref.shape[2]
    h = _rms(x_ref[0], g_ref[...]) * (1.0 + sc_ref[0]) + sh_ref[0]
    hi = h.astype(BF16)
    lo = (h - hi.astype(F32)).astype(BF16)
    logits = _dot(hi, r_ref[0]) + _dot(hi, r_ref[1]) + _dot(lo, r_ref[0])
    lane = lax.broadcasted_iota(jnp.int32, logits.shape, 1).astype(F32)
    neg = jnp.float32(-jnp.inf)
    logits = jnp.where(lane < N_EXPERTS, logits, neg)
    m1 = jnp.max(logits, axis=-1, keepdims=True)
    i1 = jnp.min(jnp.where(logits == m1, lane, float(LANES)), axis=-1, keepdims=True)
    rest = jnp.where(lane == i1, neg, logits)
    m2 = jnp.max(rest, axis=-1, keepdims=True)
    i2 = jnp.min(jnp.where(rest == m2, lane, float(LANES)), axis=-1, keepdims=True)
    e2 = jnp.exp(m2 - m1)
    w1 = 1.0 / (1.0 + e2)
    oh1 = lane == i1
    oh2 = lane == i2
    member = jnp.where(oh1 | oh2, 1.0, 0.0)
    ranks = _dot(tri_ref[...], member.astype(BF16))
    pos1 = jnp.sum(jnp.where(oh1, ranks, 0.0), axis=-1, keepdims=True)
    pos2 = jnp.sum(jnp.where(oh2, ranks, 0.0), axis=-1, keepdims=True)
    cnt_ref[0] = jnp.broadcast_to(jnp.sum(member, axis=0, keepdims=True), cnt_ref.shape[1:])
    gates = jnp.where(oh1, w1, 0.0) + jnp.where(oh2, e2 * w1, 0.0)
    g_hi = gates.astype(BF16).astype(F32)
    g_mid = (gates - g_hi).astype(BF16).astype(F32)
    g_lo = gates - g_hi - g_mid
    pieces = g_hi + pltpu.roll(g_mid, N_EXPERTS, axis=1) + pltpu.roll(g_lo, 2 * N_EXPERTS, axis=1)
    hx_ref[0, :, :d] = hi
    hx_ref[0, :, d:] = pieces.astype(BF16)
    info = jnp.where(lane == 0, i1, jnp.where(lane == 1, i2, jnp.where(
        lane == 2, pos1, jnp.where(lane == 3, pos2, 0.0))))
    info_ref[0] = info
    infot_ref[0] = info.T[:8, :]


def _router(x3, sc, sh, g, router, bpb):
    nblk, tb, d = x3.shape
    tok = lambda w: pl.BlockSpec((1, tb, w), lambda i: (i, 0, 0))
    modv = pl.BlockSpec((1, 1, d), lambda i: (i // bpb, 0, 0))
    r = jnp.arange(tb, dtype=jnp.int32)
    tri = (r[None, :] < r[:, None]).astype(BF16)
    return pl.pallas_call(
        _router_kernel,
        out_shape=(jax.ShapeDtypeStruct((nblk, tb, d + LANES), BF16),
                   jax.ShapeDtypeStruct((nblk, tb, LANES), F32),
                   jax.ShapeDtypeStruct((nblk, 8, tb), F32),
                   jax.ShapeDtypeStruct((nblk, 8, LANES), F32)),
        grid=(nblk,),
        in_specs=[tok(d), modv, modv, pl.BlockSpec((1, d), lambda i: (0, 0)),
                  pl.BlockSpec(router.shape, lambda i: (0, 0, 0)),
                  pl.BlockSpec((tb, tb), lambda i: (0, 0))],
        out_specs=(tok(d + LANES), tok(LANES), pl.BlockSpec((1, 8, tb), lambda i: (i, 0, 0)),
                   pl.BlockSpec((1, 8, LANES), lambda i: (i, 0, 0))),
        compiler_params=_params(("parallel",)),
        name="moe_router",
    )(x3, sc, sh, g, router, tri)


def _routing_tables(cnt, tm, n_tiles):
    i32 = jnp.int32
    pcnt = ((cnt + MOE_SEG - 1) // MOE_SEG) * MOE_SEG
    loc_off = jnp.cumsum(pcnt, axis=1) - pcnt
    used = jnp.sum(pcnt, axis=0)
    region = ((used + tm - 1) // tm) * tm
    ends = jnp.cumsum(region)
    base = ends - region
    off = base[None, :] + jnp.cumsum(pcnt, axis=0) - pcnt
    tile_row0 = jnp.arange(n_tiles, dtype=i32) * tm
    tile_e = jnp.minimum(jnp.sum((ends[None, :] <= tile_row0[:, None]).astype(i32), axis=1),
                         N_EXPERTS - 1)
    flat = lambda a: a.reshape(-1).astype(i32)
    zstart = jnp.concatenate([base + used, ends[-1:]])
    zrows = jnp.concatenate([region - used, n_tiles * tm - ends[-1:]])
    return dict(loc_off=flat(loc_off), nchunk=flat(pcnt // MOE_SEG), off=flat(off),
                zstart=flat(zstart), znchunk=flat(zrows // MOE_SEG),
                tile_e=tile_e.astype(i32), n_live=(ends[-1:] // tm).astype(i32))


def _segment_loops(blk, nchunk_ref, fn):
    for e in range(N_EXPERTS):
        def body(c, carry, e=e):
            fn(e, c)
            return carry
        lax.fori_loop(0, nchunk_ref[blk * N_EXPERTS + e], body, 0)


def _local_rows(idx, pos, blk, lo_ref):
    loc = pos
    for e in range(N_EXPERTS):
        loc = loc + jnp.where(idx == float(e), lo_ref[blk * N_EXPERTS + e].astype(F32), 0.0)
    return loc


def _dispatch_kernel(lo_ref, nc_ref, off_ref, zs_ref, zn_ref, hx_ref, it_ref, xs_hbm,
                     xall, zbuf, sem, zsem, *, rmax):
    i = pl.program_id(0)
    n = pl.num_programs(0)
    slot = i % 2

    def chunk_copy(blk, sl, e, c):
        k = blk * N_EXPERTS + e
        src = pl.multiple_of(lo_ref[k] + c * MOE_SEG, MOE_SEG)
        dst = pl.multiple_of(off_ref[k] + c * MOE_SEG, MOE_SEG)
        return pltpu.make_async_copy(xall.at[sl, pl.ds(src, MOE_SEG)],
                                     xs_hbm.at[pl.ds(dst, MOE_SEG)], sem.at[sl])

    def zero_copy(e, c):
        dst = pl.multiple_of(zs_ref[e] + c * MOE_SEG, MOE_SEG)
        return pltpu.make_async_copy(zbuf, xs_hbm.at[pl.ds(dst, MOE_SEG)], zsem.at[0])

    @pl.when(i >= 2)
    def _():
        _segment_loops(i - 2, nc_ref, lambda e, c: chunk_copy(i - 2, slot, e, c).wait())

    it = it_ref[0]
    loc1 = _local_rows(it[0:1, :], it[2:3, :], i, lo_ref)
    loc2 = _local_rows(it[1:2, :], it[3:4, :], i, lo_ref)
    row = lax.broadcasted_iota(jnp.int32, (rmax, it.shape[1]), 0).astype(F32)
    sel = jnp.where((row == loc1) | (row == loc2), 1.0, 0.0).astype(BF16)
    xall[slot] = _dot(sel, hx_ref[0]).astype(BF16)
    _segment_loops(i, nc_ref, lambda e, c: chunk_copy(i, slot, e, c).start())

    @pl.when(i == n - 1)
    def _():
        zbuf[...] = jnp.zeros_like(zbuf)
        for z in range(N_EXPERTS + 1):
            def zbody(c, carry, z=z):
                zero_copy(z, c).start()
                return carry
            lax.fori_loop(0, zn_ref[z], zbody, 0)

        @pl.when(n >= 2)
        def _():
            _segment_loops(i - 1, nc_ref, lambda e, c: chunk_copy(i - 1, 1 - slot, e, c).wait())
        _segment_loops(i, nc_ref, lambda e, c: chunk_copy(i, slot, e, c).wait())
        for z in range(N_EXPERTS + 1):
            def wbody(c, carry, z=z):
                zero_copy(z, c).wait()
                return carry
            lax.fori_loop(0, zn_ref[z], wbody, 0)


def _dispatch(hx, infot, tab, rows_tot, rmax):
    nblk, tb, w = hx.shape
    grid_spec = pltpu.PrefetchScalarGridSpec(
        num_scalar_prefetch=5,
        grid=(nblk,),
        in_specs=[pl.BlockSpec((1, tb, w), lambda i, *_: (i, 0, 0)),
                  pl.BlockSpec((1, 8, tb), lambda i, *_: (i, 0, 0))],
        out_specs=pl.BlockSpec(memory_space=pl.ANY),
        scratch_shapes=[pltpu.VMEM((2, rmax, w), BF16), pltpu.VMEM((MOE_SEG, w), BF16),
                        pltpu.SemaphoreType.DMA((2,)), pltpu.SemaphoreType.DMA((1,))])
    return pl.pallas_call(
        functools.partial(_dispatch_kernel, rmax=rmax),
        out_shape=jax.ShapeDtypeStruct((rows_tot, w), BF16),
        grid_spec=grid_spec,
        compiler_params=_params(("arbitrary",)),
        name="moe_dispatch",
    )(tab['loc_off'], tab['nchunk'], tab['off'], tab['zstart'], tab['znchunk'], hx, infot)


def _gffn_kernel(te_ref, nv_ref, x_ref, wg_ref, wu_ref, wd_ref, o_ref, acc_s):
    t = pl.program_id(0)
    f = pl.program_id(1)
    nf = pl.num_programs(1)
    d = o_ref.shape[1]

    @pl.when(t < nv_ref[0])
    def _():
        hb = x_ref[:, :d]
        gt = _dot(hb, wg_ref[0])
        up = _dot(hb, wu_ref[0])
        act = (gt * jax.nn.sigmoid(gt) * up).astype(BF16)
        part = _dot(act, wd_ref[0])

        @pl.when(f == 0)
        def _():
            acc_s[...] = part

        @pl.when(f > 0)
        def _():
            acc_s[...] += part

        @pl.when(f == nf - 1)
        def _():
            pieces = x_ref[:, d:].astype(F32)
            lane = lax.broadcasted_iota(jnp.int32, pieces.shape, 1)
            mine = (lane < 3 * N_EXPERTS) & (lane % N_EXPERTS == te_ref[t])
            gate = jnp.sum(jnp.where(mine, pieces, 0.0), axis=-1, keepdims=True)
            o_ref[...] = (acc_s[...] * gate).astype(BF16)

    @pl.when((t >= nv_ref[0]) & (f == nf - 1))
    def _():
        o_ref[...] = jnp.zeros_like(o_ref)


def _grouped_ffn(xs, tab, wg, wu, wd, tm, tf):
    rows, w = xs.shape
    d = wg.shape[1]
    n_tiles = rows // tm
    nf = wg.shape[2] // tf
    live = lambda t, nv: jnp.minimum(t, nv[0] - 1)
    grid_spec = pltpu.PrefetchScalarGridSpec(
        num_scalar_prefetch=2,
        grid=(n_tiles, nf),
        in_specs=[pl.BlockSpec((tm, w), lambda t, f, te, nv: (live(t, nv), 0)),
                  pl.BlockSpec((1, d, tf), lambda t, f, te, nv: (te[t], 0, f)),
                  pl.BlockSpec((1, d, tf), lambda t, f, te, nv: (te[t], 0, f)),
                  pl.BlockSpec((1, tf, d), lambda t, f, te, nv: (te[t], f, 0))],
        out_specs=pl.BlockSpec((tm, d), lambda t, f, te, nv: (t, 0)),
        scratch_shapes=[pltpu.VMEM((tm, d), F32)])
    return pl.pallas_call(
        _gffn_kernel,
        out_shape=jax.ShapeDtypeStruct((rows, d), BF16),
        grid_spec=grid_spec,
        compiler_params=_params(("arbitrary", "arbitrary")),
        name="moe_grouped_ffn",
    )(tab['tile_e'], tab['n_live'], xs, wg, wu, wd)


def _combine_kernel(lo_ref, nc_ref, off_ref, info_ref, ys_hbm, x_ref, gate_ref, gp_ref, o_ref,
                    yall, sem, *, rmax):
    i = pl.program_id(0)
    n = pl.num_programs(0)
    slot = i % 2
    tb = x_ref.shape[1]

    def chunk_copy(blk, sl, e, c):
        k = blk * N_EXPERTS + e
        src = pl.multiple_of(off_ref[k] + c * MOE_SEG, MOE_SEG)
        dst = pl.multiple_of(lo_ref[k] + c * MOE_SEG, MOE_SEG)
        return pltpu.make_async_copy(ys_hbm.at[pl.ds(src, MOE_SEG)],
                                     yall.at[sl, pl.ds(dst, MOE_SEG)], sem.at[sl])

    def fetch(blk, sl):
        yall[sl, 2 * tb:, :] = jnp.zeros((rmax - 2 * tb, yall.shape[2]), BF16)
        _segment_loops(blk, nc_ref, lambda e, c: chunk_copy(blk, sl, e, c).start())

    @pl.when(i == 0)
    def _():
        fetch(0, 0)

    @pl.when(i + 1 < n)
    def _():
        fetch(i + 1, 1 - slot)

    _segment_loops(i, nc_ref, lambda e, c: chunk_copy(i, slot, e, c).wait())
    info = info_ref[0]
    loc1 = _local_rows(info[:, 0:1], info[:, 2:3], i, lo_ref)
    loc2 = _local_rows(info[:, 1:2], info[:, 3:4], i, lo_ref)
    lane = lax.broadcasted_iota(jnp.int32, (tb, rmax), 1).astype(F32)
    sel = jnp.where((lane == loc1) | (lane == loc2), 1.0, 0.0).astype(BF16)
    o_ref[0] = x_ref[0] + gate_ref[0] * _rms(_dot(sel, yall[slot]), gp_ref[...])


def _combine(ys, info, tab, x3, gate, gp, bpb, rmax):
    nblk, tb, d = x3.shape
    grid_spec = pltpu.PrefetchScalarGridSpec(
        num_scalar_prefetch=3,
        grid=(nblk,),
        in_specs=[pl.BlockSpec((1, tb, LANES), lambda i, *_: (i, 0, 0)),
                  pl.BlockSpec(memory_space=pl.ANY),
                  pl.BlockSpec((1, tb, d), lambda i, *_: (i, 0, 0)),
                  pl.BlockSpec((1, 1, d), lambda i, *_: (i // bpb, 0, 0)),
                  pl.BlockSpec((1, d), lambda i, *_: (0, 0))],
        out_specs=pl.BlockSpec((1, tb, d), lambda i, *_: (i, 0, 0)),
        scratch_shapes=[pltpu.VMEM((2, rmax, d), BF16), pltpu.SemaphoreType.DMA((2,))])
    return pl.pallas_call(
        functools.partial(_combine_kernel, rmax=rmax),
        out_shape=jax.ShapeDtypeStruct((nblk, tb, d), F32),
        grid_spec=grid_spec,
        compiler_params=_params(("arbitrary",)),
        name="moe_combine",
    )(tab['loc_off'], tab['nchunk'], tab['off'], info, ys, x3, gate, gp)


def _moe(x, sc, sh, gpre, router, wg, wu, wd, gate, gpost, tm, tf):
    b, l, d = x.shape
    tb = min(MOE_TB, l)
    bpb = l // tb
    nblk = b * bpb
    x3 = x.reshape(nblk, tb, d)
    hx, info, infot, cnt = _router(x3, sc, sh, gpre, router, bpb)
    cnt = cnt[:, 0, :N_EXPERTS].astype(jnp.int32)
    seg_pad = N_EXPERTS * (MOE_SEG - 1)
    rmax = -(-(2 * tb + seg_pad) // 256) * 256
    n_tiles = -(-(2 * b * l + nblk * seg_pad + N_EXPERTS * (tm - 1)) // tm)
    tab = _routing_tables(cnt, tm, n_tiles)
    xs = _dispatch(hx, infot, tab, n_tiles * tm, rmax)
    ys = _grouped_ffn(xs, tab, wg, wu, wd, tm, tf)
    return _combine(ys, info, tab, x3, gate, gpost, bpb, rmax).reshape(b, l, d)


def _rope_tables(s):
    half = DIFF_DK // 2
    n_freq = half // 2
    inv = ROPE_THETA ** (-jnp.arange(n_freq, dtype=F32) / n_freq)
    t = jnp.arange(s, dtype=jnp.int32)
    rows = (t // GRID_W).astype(F32)[:, None]
    cols = (t % GRID_W).astype(F32)[:, None]
    lane = jnp.arange(LANES, dtype=jnp.int32)
    dd = lane % DIFF_DK
    pos = jnp.where((dd < half)[None, :], rows, cols)
    ang = pos * inv[dd % n_freq][None, :]
    first = ((dd % half) < n_freq)[None, :]
    cos, sin = jnp.cos(ang), jnp.sin(ang)
    return cos, jnp.where(first, -sin, 0.0), jnp.where(first, 0.0, sin)


def _dft_tables(n):
    j = jnp.arange(n, dtype=jnp.int32)
    idx = (j[:, None] * j[None, :]) % n
    ang = idx.astype(F32) * (2.0 * math.pi / n)
    sc = n ** -0.5
    return jnp.cos(ang) * sc, jnp.sin(ang) * sc


def _channel_dft():
    c, s = _dft_tables(FNET_GROUP_CH)
    eye = jnp.eye(FNET_GROUPS, dtype=F32)
    return jnp.concatenate([jnp.kron(eye, c), jnp.kron(eye, s)], axis=1).astype(BF16)


def kernel(x, c, ctx, c_ctx, ada_w, ada_b, norm_mix_pre, norm_mix_post, norm_ffn_pre, norm_ffn_post,
           w_in, w_out, diff_lq1, diff_lk1, diff_lq2, diff_lk2, diff_subln, fnet_w,
           s5_a_re, s5_a_im, s5_log_dt, s5_b_re, s5_b_im, s5_c_re, s5_c_im, s5_d, s5_w_glu,
           ffn_w_gate, ffn_w_up, ffn_w_down, moe_router, moe_w_gate, moe_w_up, moe_w_down):
    b, s, d = x.shape
    lc = ctx.shape[1]
    depth = ada_w.shape[0]

    cc = jnp.zeros((MOD_ROWS, d), F32).at[:b].set(c).at[b].set(c_ctx)
    mod = _modulation(cc, ada_w, ada_b)

    rope_tabs = _rope_tables(s)
    cs64 = _channel_dft()
    dft = {n: tuple(t.astype(BF16) for t in _dft_tables(n)) for n in (s, lc)}

    xc = ctx
    for l in range(depth):
        need_ctx = l < depth - 1
        lam_init = 0.8 - 0.6 * math.exp(-0.3 * l)
        m_lat = [mod[l, :b, i * d:(i + 1) * d].reshape(b, 1, d) for i in range(6)]
        m_ctx = [jnp.broadcast_to(mod[l, b, i * d:(i + 1) * d].reshape(1, 1, d), (b, 1, d))
                 for i in range(6)]
        row = lambda v: v.reshape(1, -1).astype(F32)
        w_in_bf = w_in[l].astype(BF16)
        w_out_bf = w_out[l].astype(BF16)
        fw_bf = fnet_w[l].astype(BF16)
        wglu_bf = s5_w_glu[l].astype(BF16)
        lams = (row(diff_lq1[l]), row(diff_lk1[l]), row(diff_lq2[l]), row(diff_lk2[l]))
        subln = row(diff_subln[l])

        q, k, v, g1, g2, u = _inproj(x, m_lat[1], m_lat[0], row(norm_mix_pre[l]), w_in_bf, cs64,
                                     rope_tabs, min(512, s))
        qc, kc, vc, g1c, g2c, uc = _inproj(xc, m_ctx[1], m_ctx[0], row(norm_mix_pre[l]), w_in_bf,
                                           cs64, None, lc)
        a_lat = _attention(q, [(kc, vc), (k, v)], lams, subln, lam_init, min(512, s))
        f_lat = _fnet_dft(*dft[s], g1, g2, fw_bf, b, min(512, s), 2)

        mats = [_s5_matrices(s5_a_re[l, dr], s5_a_im[l, dr], s5_log_dt[l, dr], s5_b_re[l, dr],
                             s5_b_im[l, dr], s5_c_re[l, dr], s5_c_im[l, dr]) for dr in (0, 1)]
        mats = (mats[0][0], mats[1][0], mats[0][1], mats[1][1], mats[0][2], mats[1][2])
        zero_state = jnp.zeros((b, 2 * S5_LANES), F32)
        ycf, ycr, hcf, hcr = _s5_scan(uc, mats, zero_state, zero_state, b, S5_TC)
        yf, yr, _, _ = _s5_scan(u, mats, hcf, hcr, b, S5_TC)

        x = _outproj(a_lat, f_lat, yf, yr, u, row(s5_d[l]), wglu_bf, w_out_bf,
                     x, m_lat[2], row(norm_mix_post[l]), min(512, s))
        if need_ctx:
            a_ctx = _attention(qc, [(kc, vc)], lams, subln, lam_init, lc)
            f_ctx = _fnet_dft(*dft[lc], g1c, g2c, fw_bf, b, lc, 2)
            xc = _outproj(a_ctx, f_ctx, ycf, ycr, uc, row(s5_d[l]), wglu_bf,
                          w_out_bf, xc, m_ctx[2], row(norm_mix_post[l]), lc)

        i = l // 2
        gpre, gpost = row(norm_ffn_pre[l]), row(norm_ffn_post[l])
        xc1 = xc.reshape(1, b * lc, d)
        if l % 2 == 0:
            wg, wu, wd = (ffn_w_gate[i].astype(BF16), ffn_w_up[i].astype(BF16),
                          ffn_w_down[i].astype(BF16))
            tf = FFN_DENSE_TF
            x = _ffn(x, m_lat[4], m_lat[3], gpre, wg, wu, wd, m_lat[5], gpost, min(1024, s), tf)
            if need_ctx:
                xc1 = _ffn(xc1, m_ctx[4][:1], m_ctx[3][:1], gpre, wg, wu, wd, m_ctx[5][:1], gpost,
                           min(1024, b * lc), tf)
        else:
            wg, wu, wd = (moe_w_gate[i].astype(BF16), moe_w_up[i].astype(BF16),
                          moe_w_down[i].astype(BF16))
            r = jnp.zeros((d, LANES), F32).at[:, :N_EXPERTS].set(moe_router[i])
            r_hi = r.astype(BF16)
            router = jnp.stack([r_hi, (r - r_hi.astype(F32)).astype(BF16)])
            x = _moe(x, m_lat[4], m_lat[3], gpre, router, wg, wu, wd, m_lat[5], gpost,
                     MOE_TM, MOE_TF)
            if need_ctx:
                xc1 = _moe(xc1, m_ctx[4][:1], m_ctx[3][:1], gpre, router, wg, wu, wd, m_ctx[5][:1],
                           gpost, MOE_TM, MOE_TF)
        xc = xc1.reshape(b, lc, d)
    return x
```

```python
import functools
import math

import jax
import jax.numpy as jnp
from jax import lax
from jax.experimental import pallas as pl
from jax.experimental.pallas import tpu as pltpu

F32 = jnp.float32
BF16 = jnp.bfloat16

D_MODEL = 1024
DEPTH = 2
GRID_W = 64
EPS = 1e-6
DIFF_HEADS = 4
DIFF_DK = 64
DIFF_DV = 2 * DIFF_DK
DIFF_WIDTH = DIFF_HEADS * DIFF_DV
DIFF_QK_WIDTH = DIFF_HEADS * 2 * DIFF_DK
ROPE_THETA = 10000.0
FNET_GROUPS = 4
FNET_GROUP_CH = 64
FNET_WIDTH = FNET_GROUPS * FNET_GROUP_CH
S5_CH = 16
S5_GROUPS = 16
S5_STATE = 64
S5_WIDTH = S5_GROUPS * S5_CH
S5_LANES = S5_GROUPS * S5_STATE
IN_WIDTH = 2 * DIFF_QK_WIDTH + DIFF_WIDTH + FNET_WIDTH + S5_WIDTH
N_EXPERTS = 8
LANES = 128
LOG2E = math.log2(math.e)
MOD_ROWS = 24
FFN_DENSE_TF = 1408
INPROJ_TL = 1024
INPROJ_RSUB = 256
ATTN_TQ = 2048
ATTN_RSUB = 128
S5_TC = 32
MOE_TB = 1024
MOE_SEG = 32
MOE_TM = 1024
MOE_TF = 896

VMEM_LIMIT = 56 * 1024 * 1024


def _params(sem):
    return pltpu.CompilerParams(dimension_semantics=sem, vmem_limit_bytes=VMEM_LIMIT)


def _rms(x, g):
    return x * lax.rsqrt(jnp.mean(x * x, axis=-1, keepdims=True) + EPS) * g


def _dot(a, b):
    return jnp.dot(a, b, preferred_element_type=F32)


def _mod_kernel(c_ref, w_ref, b_ref, o_ref):
    c = c_ref[...]
    sc = c * jax.nn.sigmoid(c)
    o_ref[0] = jnp.dot(sc, w_ref[0], preferred_element_type=F32,
                       precision=lax.Precision.HIGHEST) + b_ref[0]


def _modulation(cc, ada_w, ada_b):
    depth, d, n = ada_w.shape
    tn = 1536
    return pl.pallas_call(
        _mod_kernel,
        out_shape=jax.ShapeDtypeStruct((depth, MOD_ROWS, n), F32),
        grid=(depth, n // tn),
        in_specs=[pl.BlockSpec((MOD_ROWS, d), lambda l, j: (0, 0)),
                  pl.BlockSpec((1, d, tn), lambda l, j: (l, 0, j)),
                  pl.BlockSpec((1, 1, tn), lambda l, j: (l, 0, j))],
        out_specs=pl.BlockSpec((1, MOD_ROWS, tn), lambda l, j: (l, 0, j)),
        compiler_params=_params(("parallel", "parallel")),
        name="adaln_mod",
    )(cc, ada_w, ada_b.reshape(depth, 1, n))


def _inproj_kernel(*refs, rope):
    if rope:
        (x_ref, sc_ref, sh_ref, g_ref, w_ref, cs_ref, cos_ref, sina_ref, sinb_ref,
         q_ref, k_ref, v_ref, g1_ref, g2_ref, u_ref) = refs
    else:
        (x_ref, sc_ref, sh_ref, g_ref, w_ref, cs_ref,
         q_ref, k_ref, v_ref, g1_ref, g2_ref, u_ref) = refs
    tl = x_ref.shape[1]
    rsub = min(INPROJ_RSUB, tl)
    for r0 in range(0, tl, rsub):
        rows = slice(r0, r0 + rsub)
        h = _rms(x_ref[0, rows, :], g_ref[...]) * (1.0 + sc_ref[0]) + sh_ref[0]
        hb = h.astype(BF16)

        def proj(lo, hi):
            return _dot(hb, w_ref[:, lo:hi])

        def rotate(t):
            outs = []
            for j in range(t.shape[1] // LANES):
                tb = t[:, j * LANES:(j + 1) * LANES]
                outs.append(tb * cos_ref[rows, :]
                            + pltpu.roll(tb, LANES - 16, axis=1) * sina_ref[rows, :]
                            + pltpu.roll(tb, 16, axis=1) * sinb_ref[rows, :])
            return jnp.concatenate(outs, axis=1)

        q = proj(0, DIFF_QK_WIDTH)
        k = proj(DIFF_QK_WIDTH, 2 * DIFF_QK_WIDTH)
        if rope:
            q = rotate(q)
            k = rotate(k)
        q_ref[0, rows, :] = (q * (DIFF_DK ** -0.5 * LOG2E)).astype(BF16)
        k_ref[0, rows, :] = k.astype(BF16)
        o = 2 * DIFF_QK_WIDTH
        v_ref[0, rows, :] = proj(o, o + DIFF_WIDTH).astype(BF16)
        o += DIFF_WIDTH
        f = proj(o, o + FNET_WIDTH).astype(BF16)
        g12 = _dot(f, cs_ref[...])
        g1_ref[rows, :] = g12[:, :FNET_WIDTH].astype(BF16)
        g2_ref[rows, :] = g12[:, FNET_WIDTH:].astype(BF16)
        u_ref[rows, :] = proj(o + FNET_WIDTH, IN_WIDTH)


def _inproj(x, sc, sh, g, w_bf, cs64, rope_tabs, tl):
    b, l, d = x.shape
    rope = rope_tabs is not None
    in_specs = [pl.BlockSpec((1, tl, d), lambda i, bb: (bb, i, 0)),
                pl.BlockSpec((1, 1, d), lambda i, bb: (bb, 0, 0)),
                pl.BlockSpec((1, 1, d), lambda i, bb: (bb, 0, 0)),
                pl.BlockSpec((1, d), lambda i, bb: (0, 0)),
                pl.BlockSpec((d, IN_WIDTH), lambda i, bb: (0, 0)),
                pl.BlockSpec((FNET_WIDTH, 2 * FNET_WIDTH), lambda i, bb: (0, 0))]
    args = [x, sc, sh, g, w_bf, cs64]
    if rope:
        in_specs += [pl.BlockSpec((tl, LANES), lambda i, bb: (i, 0))] * 3
        args += list(rope_tabs)
    tok = lambda w: pl.BlockSpec((1, tl, w), lambda i, bb: (bb, i, 0))
    tb = pl.BlockSpec((tl, FNET_WIDTH), lambda i, bb: (i, bb))
    return pl.pallas_call(
        functools.partial(_inproj_kernel, rope=rope),
        out_shape=(jax.ShapeDtypeStruct((b, l, DIFF_QK_WIDTH), BF16),
                   jax.ShapeDtypeStruct((b, l, DIFF_QK_WIDTH), BF16),
                   jax.ShapeDtypeStruct((b, l, DIFF_WIDTH), BF16),
                   jax.ShapeDtypeStruct((l, b * FNET_WIDTH), BF16),
                   jax.ShapeDtypeStruct((l, b * FNET_WIDTH), BF16),
                   jax.ShapeDtypeStruct((l, b * S5_WIDTH), F32)),
        grid=(l // tl, b),
        in_specs=in_specs,
        out_specs=(tok(DIFF_QK_WIDTH), tok(DIFF_QK_WIDTH), tok(DIFF_WIDTH), tb, tb, tb),
        compiler_params=_params(("parallel", "parallel")),
        name="inproj_rope" if rope else "inproj",
    )(*args)


def _attn_kernel(*refs, n_src, lam_init, rsub):
    q_ref = refs[0]
    kv = refs[1:1 + 2 * n_src]
    lq1, lk1, lq2, lk2, sub_ref, o_ref, k1_s, k2_s, v_s = refs[1 + 2 * n_src:]

    @pl.when(pl.program_id(2) == 0)
    def _():
        off = 0
        for s in range(n_src):
            kk = kv[2 * s][0]
            n = kk.shape[0]
            k1_s[off:off + n, :] = kk[:, :DIFF_DK]
            k2_s[off:off + n, :] = kk[:, DIFF_DK:]
            v_s[off:off + n, :DIFF_DV] = kv[2 * s + 1][0]
            off += n
        v_s[:, DIFF_DV:] = jnp.ones((v_s.shape[0], DIFF_DV), BF16)

    lam = (jnp.exp(jnp.sum(lq1[...] * lk1[...], axis=-1, keepdims=True))
           - jnp.exp(jnp.sum(lq2[...] * lk2[...], axis=-1, keepdims=True)) + lam_init)
    def attend(qj, k_s):
        s = lax.dot_general(qj, k_s[...], (((1,), (1,)), ((), ())), preferred_element_type=F32)
        p = jnp.exp2(s - jnp.max(s, axis=-1, keepdims=True)).astype(BF16)
        ol = _dot(p, v_s[...])
        return ol[:, :DIFF_DV] / ol[:, DIFF_DV:DIFF_DV + 1]

    for r0 in range(0, q_ref.shape[1], rsub):
        q = q_ref[0, r0:r0 + rsub, :]
        o = attend(q[:, :DIFF_DK], k1_s) - lam * attend(q[:, DIFF_DK:], k2_s)
        o_ref[0, r0:r0 + rsub, :] = (_rms(o, sub_ref[...]) * (1.0 - lam_init)).astype(BF16)


def _attention(q, kv_srcs, lams, subln, lam_init, tq):
    b, lq, _ = q.shape
    n_src = len(kv_srcs)
    lk = sum(k.shape[1] for k, _ in kv_srcs)
    in_specs = [pl.BlockSpec((1, tq, DIFF_DV), lambda bb, h, i: (bb, i, h))]
    args = [q]
    for k, v in kv_srcs:
        spec = pl.BlockSpec((1, k.shape[1], DIFF_DV), lambda bb, h, i: (bb, 0, h))
        in_specs += [spec, spec]
        args += [k, v]
    in_specs += [pl.BlockSpec((1, DIFF_DK), lambda bb, h, i: (0, 0))] * 4
    in_specs += [pl.BlockSpec((1, DIFF_DV), lambda bb, h, i: (0, 0))]
    args += list(lams) + [subln]
    return pl.pallas_call(
        functools.partial(_attn_kernel, n_src=n_src, lam_init=lam_init, rsub=min(ATTN_RSUB, tq)),
        out_shape=jax.ShapeDtypeStruct((b, lq, DIFF_WIDTH), BF16),
        grid=(b, DIFF_HEADS, lq // tq),
        in_specs=in_specs,
        out_specs=pl.BlockSpec((1, tq, DIFF_DV), lambda bb, h, i: (bb, i, h)),
        scratch_shapes=[pltpu.VMEM((lk, DIFF_DK), BF16), pltpu.VMEM((lk, DIFF_DK), BF16),
                        pltpu.VMEM((lk, 2 * DIFF_DV), BF16)],
        compiler_params=_params(("parallel", "parallel", "arbitrary")),
        name="diff_attn_%d" % n_src,
    )(*args)


def _dft_kernel(c_ref, s_ref, g1_ref, g2_ref, w_ref, o_ref):
    z = _dot(c_ref[...], g1_ref[...]) - _dot(s_ref[...], g2_ref[...])
    for j in range(o_ref.shape[0]):
        zj = z[:, j * FNET_WIDTH:(j + 1) * FNET_WIDTH].astype(BF16)
        o_ref[j] = _dot(zj, w_ref[...]).astype(BF16)


def _fnet_dft(cosm, sinm, g1, g2, w_bf, b, tm, nb):
    l = cosm.shape[0]
    tn = nb * FNET_WIDTH
    return pl.pallas_call(
        _dft_kernel,
        out_shape=jax.ShapeDtypeStruct((b, l, FNET_WIDTH), BF16),
        grid=(l // tm, b // nb),
        in_specs=[pl.BlockSpec((tm, l), lambda i, j: (i, 0)),
                  pl.BlockSpec((tm, l), lambda i, j: (i, 0)),
                  pl.BlockSpec((l, tn), lambda i, j: (0, j)),
                  pl.BlockSpec((l, tn), lambda i, j: (0, j)),
                  pl.BlockSpec((FNET_WIDTH, FNET_WIDTH), lambda i, j: (0, 0))],
        out_specs=pl.BlockSpec((nb, tm, FNET_WIDTH), lambda i, j: (j, i, 0)),
        compiler_params=_params(("parallel", "parallel")),
        name="fnet_dft",
    )(cosm, sinm, g1, g2, w_bf)


def _s5_kernel(uf_ref, ur_ref, perm_ref, permt_ref, bf_ref, br_ref, lf_ref, lr_ref, cf_ref, cr_ref,
               h0f_ref, h0r_ref, yf_ref, yr_ref, hef_ref, her_ref, hs_f, hs_r, hb_f, hb_r,
               *, tc, nb):
    j = pl.program_id(0)

    @pl.when(j == 0)
    def _():
        hs_f[...] = h0f_ref[...]
        hs_r[...] = h0r_ref[...]

    def drive(u_ref, b_ref):
        u_bt = jnp.concatenate(
            [u_ref[:, bb * S5_WIDTH:(bb + 1) * S5_WIDTH] for bb in range(nb)], axis=0).astype(BF16)
        u_tb = _dot(perm_ref[...], u_bt).astype(BF16)
        return _dot(u_tb, b_ref[...])

    def scan(drv, l_ref, hs, hb, reverse):
        lre = l_ref[:, :S5_LANES]
        lim = l_ref[:, S5_LANES:]
        hr, hi = hs[:, :S5_LANES], hs[:, S5_LANES:]
        for t in (range(tc - 1, -1, -1) if reverse else range(tc)):
            d = drv[t * nb:(t + 1) * nb, :]
            hr, hi = (lre * hr - lim * hi + d[:, :S5_LANES], lre * hi + lim * hr + d[:, S5_LANES:])
            hb[t * nb:(t + 1) * nb, :S5_LANES] = hr.astype(BF16)
            hb[t * nb:(t + 1) * nb, S5_LANES:] = hi.astype(BF16)
        hs[:, :S5_LANES] = hr
        hs[:, S5_LANES:] = hi

    def readout(hb, c_ref, y_ref):
        y = _dot(hb[...], c_ref[...])
        y_hi = y.astype(BF16)
        y_lo = (y - y_hi.astype(F32)).astype(BF16)
        y_bt = _dot(permt_ref[...], y_hi) + _dot(permt_ref[...], y_lo)
        for bb in range(nb):
            y_ref[:, bb * S5_WIDTH:(bb + 1) * S5_WIDTH] = y_bt[bb * tc:(bb + 1) * tc, :]

    drv_f = drive(uf_ref, bf_ref)
    drv_r = drive(ur_ref, br_ref)
    scan(drv_f, lf_ref, hs_f, hb_f, False)
    scan(drv_r, lr_ref, hs_r, hb_r, True)
    readout(hb_f, cf_ref, yf_ref)
    readout(hb_r, cr_ref, yr_ref)

    @pl.when(j == pl.num_programs(0) - 1)
    def _():
        hef_ref[...] = hs_f[...]
        her_ref[...] = hs_r[...]


def _s5_scan(u, mats, h0f, h0r, nb, tc):
    l = u.shape[0]
    n = l // tc
    r = tc * nb
    bmf, bmr, lf, lr, cmf, cmr = mats
    rows = jnp.arange(r, dtype=jnp.int32)
    perm = (rows[None, :] == ((rows % nb) * tc + rows // nb)[:, None]).astype(BF16)
    permt = perm.T
    full = lambda a: pl.BlockSpec(a.shape, lambda j: (0,) * a.ndim)
    fwd = pl.BlockSpec((tc, nb * S5_WIDTH), lambda j: (j, 0))
    rev = pl.BlockSpec((tc, nb * S5_WIDTH), lambda j: (n - 1 - j, 0))
    st = jax.ShapeDtypeStruct((nb, 2 * S5_LANES), F32)
    return pl.pallas_call(
        functools.partial(_s5_kernel, tc=tc, nb=nb),
        out_shape=(jax.ShapeDtypeStruct(u.shape, F32), jax.ShapeDtypeStruct(u.shape, F32), st, st),
        grid=(n,),
        in_specs=[fwd, rev, full(perm), full(permt), full(bmf), full(bmr), full(lf), full(lr),
                  full(cmf), full(cmr), full(h0f), full(h0r)],
        out_specs=(fwd, rev, full(h0f), full(h0r)),
        scratch_shapes=[pltpu.VMEM((nb, 2 * S5_LANES), F32), pltpu.VMEM((nb, 2 * S5_LANES), F32),
                        pltpu.VMEM((r, 2 * S5_LANES), BF16), pltpu.VMEM((r, 2 * S5_LANES), BF16)],
        compiler_params=_params(("arbitrary",)),
        name="s5_scan",
    )(u, u, perm, permt, bmf, bmr, lf, lr, cmf, cmr, h0f, h0r)


def _s5_matrices(a_re, a_im, log_dt, b_re, b_im, c_re, c_im):
    dt = jnp.exp(log_dt)[:, None]
    mag = jnp.exp(a_re * dt)
    lr, li = mag * jnp.cos(a_im * dt), mag * jnp.sin(a_im * dt)
    nr, ni = lr - 1.0, li
    den = a_re * a_re + a_im * a_im
    cr = (nr * a_re + ni * a_im) / den
    ci = (ni * a_re - nr * a_im) / den
    bbr = cr[..., None] * b_re - ci[..., None] * b_im
    bbi = cr[..., None] * b_im + ci[..., None] * b_re
    eye = jnp.eye(S5_GROUPS, dtype=F32)

    def drive_mat(bb):
        return jnp.einsum('gpc,gh->gchp', bb, eye).reshape(S5_WIDTH, S5_LANES)

    def read_mat(cc):
        return jnp.einsum('gcp,gh->gphc', cc, eye).reshape(S5_LANES, S5_WIDTH)

    bm = jnp.concatenate([drive_mat(bbr), drive_mat(bbi)], axis=1).astype(BF16)
    cm = jnp.concatenate([read_mat(c_re), -read_mat(c_im)], axis=0).astype(BF16)
    lam = jnp.concatenate([lr.reshape(1, S5_LANES), li.reshape(1, S5_LANES)], axis=1)
    return bm, lam, cm


def _outproj_kernel(a_ref, f_ref, yf_ref, yr_ref, u_ref, d_ref, wglu_ref, wa_ref, wf_ref, ws_ref,
                    x_ref, gate_ref, g_ref, o_ref):
    yy = yf_ref[...] + yr_ref[...] + d_ref[...] * u_ref[...]
    yy = jax.nn.gelu(yy)
    s = yy * jax.nn.sigmoid(_dot(yy.astype(BF16), wglu_ref[...]))
    mix = (_dot(a_ref[0], wa_ref[...]) + _dot(f_ref[0], wf_ref[...])
           + _dot(s.astype(BF16), ws_ref[...]))
    o_ref[0] = x_ref[0] + gate_ref[0] * _rms(mix, g_ref[...])


def _outproj(a, fo, yf, yr, u, d, wglu, w_out, x, gate, g, tl):
    b, l, dm = x.shape
    wa, wf, ws = (w_out[:DIFF_WIDTH], w_out[DIFF_WIDTH:DIFF_WIDTH + FNET_WIDTH],
                  w_out[DIFF_WIDTH + FNET_WIDTH:])
    tok = lambda w: pl.BlockSpec((1, tl, w), lambda i, bb: (bb, i, 0))
    tb = pl.BlockSpec((tl, S5_WIDTH), lambda i, bb: (i, bb))
    full = lambda arr: pl.BlockSpec(arr.shape, lambda i, bb: (0,) * arr.ndim)
    return pl.pallas_call(
        _outproj_kernel,
        out_shape=jax.ShapeDtypeStruct((b, l, dm), F32),
        grid=(l // tl, b),
        in_specs=[tok(DIFF_WIDTH), tok(FNET_WIDTH), tb, tb, tb, full(d), full(wglu),
                  full(wa), full(wf), full(ws), tok(dm),
                  pl.BlockSpec((1, 1, dm), lambda i, bb: (bb, 0, 0)), full(g)],
        out_specs=tok(dm),
        compiler_params=_params(("parallel", "parallel")),
        name="outproj",
    )(a, fo, yf, yr, u, d, wglu, wa, wf, ws, x, gate, g)


def _ffn_kernel(x_ref, sc_ref, sh_ref, g_ref, wg_ref, wu_ref, wd_ref, gate_ref, gp_ref,
                o_ref, h_s, acc_s):
    f = pl.program_id(2)

    @pl.when(f == 0)
    def _():
        h = _rms(x_ref[0], g_ref[...]) * (1.0 + sc_ref[0]) + sh_ref[0]
        h_s[...] = h.astype(BF16)
        acc_s[...] = jnp.zeros_like(acc_s)

    hb = h_s[...]
    gt = _dot(hb, wg_ref[...])
    up = _dot(hb, wu_ref[...])
    act = (gt * jax.nn.sigmoid(gt) * up).astype(BF16)
    acc_s[...] += _dot(act, wd_ref[...])

    @pl.when(f == pl.num_programs(2) - 1)
    def _():
        o_ref[0] = x_ref[0] + gate_ref[0] * _rms(acc_s[...], gp_ref[...])


def _ffn(x, sc, sh, g, wg, wu, wd, gate, gp, tm, tf):
    b, l, d = x.shape
    fdim = wg.shape[1]
    tok = pl.BlockSpec((1, tm, d), lambda bb, i, f: (bb, i, 0))
    modv = pl.BlockSpec((1, 1, d), lambda bb, i, f: (bb, 0, 0))
    vec = pl.BlockSpec((1, d), lambda bb, i, f: (0, 0))
    return pl.pallas_call(
        _ffn_kernel,
        out_shape=jax.ShapeDtypeStruct((b, l, d), F32),
        grid=(b, l // tm, fdim // tf),
        in_specs=[tok, modv, modv, vec,
                  pl.BlockSpec((d, tf), lambda bb, i, f: (0, f)),
                  pl.BlockSpec((d, tf), lambda bb, i, f: (0, f)),
                  pl.BlockSpec((tf, d), lambda bb, i, f: (f, 0)),
                  modv, vec],
        out_specs=tok,
        scratch_shapes=[pltpu.VMEM((tm, d), BF16), pltpu.VMEM((tm, d), F32)],
        compiler_params=_params(("parallel", "parallel", "arbitrary")),
        name="dense_ffn",
    )(x, sc, sh, g, wg, wu, wd, gate, gp)


def _router_kernel(x_ref, sc_ref, sh_ref, g_ref, r_ref, tri_ref, hx_ref, info_ref, infot_ref, cnt_ref):
    d = x_ref.shape[2]
    h = _rms(x_ref[0], g_ref[...]) * (1.0 + sc_ref[0]) + sh_ref[0]
    hi = h.astype(BF16)
    lo = (h - hi.astype(F32)).astype(BF16)
    logits = _dot(hi, r_ref[0]) + _dot(hi, r_ref[1]) + _dot(lo, r_ref[0])
    lane = lax.broadcasted_iota(jnp.int32, logits.shape, 1).astype(F32)
    neg = jnp.float32(-jnp.inf)
    logits = jnp.where(lane < N_EXPERTS, logits, neg)
    m1 = jnp.max(logits, axis=-1, keepdims=True)
    i1 = jnp.min(jnp.where(logits == m1, lane, float(LANES)), axis=-1, keepdims=True)
    rest = jnp.where(lane == i1, neg, logits)
    m2 = jnp.max(rest, axis=-1, keepdims=True)
    i2 = jnp.min(jnp.where(rest == m2, lane, float(LANES)), axis=-1, keepdims=True)
    e2 = jnp.exp(m2 - m1)
    w1 = 1.0 / (1.0 + e2)
    oh1 = lane == i1
    oh2 = lane == i2
    member = jnp.where(oh1 | oh2, 1.0, 0.0)
    ranks = _dot(tri_ref[...], member.astype(BF16))
    pos1 = jnp.sum(jnp.where(oh1, ranks, 0.0), axis=-1, keepdims=True)
    pos2 = jnp.sum(jnp.where(oh2, ranks, 0.0), axis=-1, keepdims=True)
    cnt_ref[0] = jnp.broadcast_to(jnp.sum(member, axis=0, keepdims=True), cnt_ref.shape[1:])
    gates = jnp.where(oh1, w1, 0.0) + jnp.where(oh2, e2 * w1, 0.0)
    g_hi = gates.astype(BF16).astype(F32)
    g_mid = (gates - g_hi).astype(BF16).astype(F32)
    g_lo = gates - g_hi - g_mid
    pieces = g_hi + pltpu.roll(g_mid, N_EXPERTS, axis=1) + pltpu.roll(g_lo, 2 * N_EXPERTS, axis=1)
    hx_ref[0, :, :d] = hi
    hx_ref[0, :, d:] = pieces.astype(BF16)
    info = jnp.where(lane == 0, i1, jnp.where(lane == 1, i2, jnp.where(
        lane == 2, pos1, jnp.where(lane == 3, pos2, 0.0))))
    info_ref[0] = info
    infot_ref[0] = info.T[:8, :]


def _router(x3, sc, sh, g, router, bpb):
    nblk, tb, d = x3.shape
    tok = lambda w: pl.BlockSpec((1, tb, w), lambda i: (i, 0, 0))
    modv = pl.BlockSpec((1, 1, d), lambda i: (i // bpb, 0, 0))
    r = jnp.arange(tb, dtype=jnp.int32)
    tri = (r[None, :] < r[:, None]).astype(BF16)
    return pl.pallas_call(
        _router_kernel,
        out_shape=(jax.ShapeDtypeStruct((nblk, tb, d + LANES), BF16),
                   jax.ShapeDtypeStruct((nblk, tb, LANES), F32),
                   jax.ShapeDtypeStruct((nblk, 8, tb), F32),
                   jax.ShapeDtypeStruct((nblk, 8, LANES), F32)),
        grid=(nblk,),
        in_specs=[tok(d), modv, modv, pl.BlockSpec((1, d), lambda i: (0, 0)),
                  pl.BlockSpec(router.shape, lambda i: (0, 0, 0)),
                  pl.BlockSpec((tb, tb), lambda i: (0, 0))],
        out_specs=(tok(d + LANES), tok(LANES), pl.BlockSpec((1, 8, tb), lambda i: (i, 0, 0)),
                   pl.BlockSpec((1, 8, LANES), lambda i: (i, 0, 0))),
        compiler_params=_params(("parallel",)),
        name="moe_router",
    )(x3, sc, sh, g, router, tri)


def _routing_tables(cnt, tm, n_tiles):
    i32 = jnp.int32
    pcnt = ((cnt + MOE_SEG - 1) // MOE_SEG) * MOE_SEG
    loc_off = jnp.cumsum(pcnt, axis=1) - pcnt
    used = jnp.sum(pcnt, axis=0)
    region = ((used + tm - 1) // tm) * tm
    ends = jnp.cumsum(region)
    base = ends - region
    off = base[None, :] + jnp.cumsum(pcnt, axis=0) - pcnt
    tile_row0 = jnp.arange(n_tiles, dtype=i32) * tm
    tile_e = jnp.minimum(jnp.sum((ends[None, :] <= tile_row0[:, None]).astype(i32), axis=1),
                         N_EXPERTS - 1)
    flat = lambda a: a.reshape(-1).astype(i32)
    zstart = jnp.concatenate([base + used, ends[-1:]])
    zrows = jnp.concatenate([region - used, n_tiles * tm - ends[-1:]])
    return dict(loc_off=flat(loc_off), nchunk=flat(pcnt // MOE_SEG), off=flat(off),
                zstart=flat(zstart), znchunk=flat(zrows // MOE_SEG),
                tile_e=tile_e.astype(i32), n_live=(ends[-1:] // tm).astype(i32))


def _segment_loops(blk, nchunk_ref, fn):
    for e in range(N_EXPERTS):
        def body(c, carry, e=e):
            fn(e, c)
            return carry
        lax.fori_loop(0, nchunk_ref[blk * N_EXPERTS + e], body, 0)


def _local_rows(idx, pos, blk, lo_ref):
    loc = pos
    for e in range(N_EXPERTS):
        loc = loc + jnp.where(idx == float(e), lo_ref[blk * N_EXPERTS + e].astype(F32), 0.0)
    return loc


def _dispatch_kernel(lo_ref, nc_ref, off_ref, zs_ref, zn_ref, hx_ref, it_ref, xs_hbm,
                     xall, zbuf, sem, zsem, *, rmax):
    i = pl.program_id(0)
    n = pl.num_programs(0)
    slot = i % 2

    def chunk_copy(blk, sl, e, c):
        k = blk * N_EXPERTS + e
        src = pl.multiple_of(lo_ref[k] + c * MOE_SEG, MOE_SEG)
        dst = pl.multiple_of(off_ref[k] + c * MOE_SEG, MOE_SEG)
        return pltpu.make_async_copy(xall.at[sl, pl.ds(src, MOE_SEG)],
                                     xs_hbm.at[pl.ds(dst, MOE_SEG)], sem.at[sl])

    def zero_copy(e, c):
        dst = pl.multiple_of(zs_ref[e] + c * MOE_SEG, MOE_SEG)
        return pltpu.make_async_copy(zbuf, xs_hbm.at[pl.ds(dst, MOE_SEG)], zsem.at[0])

    @pl.when(i >= 2)
    def _():
        _segment_loops(i - 2, nc_ref, lambda e, c: chunk_copy(i - 2, slot, e, c).wait())

    it = it_ref[0]
    loc1 = _local_rows(it[0:1, :], it[2:3, :], i, lo_ref)
    loc2 = _local_rows(it[1:2, :], it[3:4, :], i, lo_ref)
    row = lax.broadcasted_iota(jnp.int32, (rmax, it.shape[1]), 0).astype(F32)
    sel = jnp.where((row == loc1) | (row == loc2), 1.0, 0.0).astype(BF16)
    xall[slot] = _dot(sel, hx_ref[0]).astype(BF16)
    _segment_loops(i, nc_ref, lambda e, c: chunk_copy(i, slot, e, c).start())

    @pl.when(i == n - 1)
    def _():
        zbuf[...] = jnp.zeros_like(zbuf)
        for z in range(N_EXPERTS + 1):
            def zbody(c, carry, z=z):
                zero_copy(z, c).start()
                return carry
            lax.fori_loop(0, zn_ref[z], zbody, 0)

        @pl.when(n >= 2)
        def _():
            _segment_loops(i - 1, nc_ref, lambda e, c: chunk_copy(i - 1, 1 - slot, e, c).wait())
        _segment_loops(i, nc_ref, lambda e, c: chunk_copy(i, slot, e, c).wait())
        for z in range(N_EXPERTS + 1):
            def wbody(c, carry, z=z):
                zero_copy(z, c).wait()
                return carry
            lax.fori_loop(0, zn_ref[z], wbody, 0)


def _dispatch(hx, infot, tab, rows_tot, rmax):
    nblk, tb, w = hx.shape
    grid_spec = pltpu.PrefetchScalarGridSpec(
        num_scalar_prefetch=5,
        grid=(nblk,),
        in_specs=[pl.BlockSpec((1, tb, w), lambda i, *_: (i, 0, 0)),
                  pl.BlockSpec((1, 8, tb), lambda i, *_: (i, 0, 0))],
        out_specs=pl.BlockSpec(memory_space=pl.ANY),
        scratch_shapes=[pltpu.VMEM((2, rmax, w), BF16), pltpu.VMEM((MOE_SEG, w), BF16),
                        pltpu.SemaphoreType.DMA((2,)), pltpu.SemaphoreType.DMA((1,))])
    return pl.pallas_call(
        functools.partial(_dispatch_kernel, rmax=rmax),
        out_shape=jax.ShapeDtypeStruct((rows_tot, w), BF16),
        grid_spec=grid_spec,
        compiler_params=_params(("arbitrary",)),
        name="moe_dispatch",
    )(tab['loc_off'], tab['nchunk'], tab['off'], tab['zstart'], tab['znchunk'], hx, infot)


def _gffn_kernel(te_ref, nv_ref, x_ref, wg_ref, wu_ref, wd_ref, o_ref, acc_s):
    t = pl.program_id(0)
    f = pl.program_id(1)
    nf = pl.num_programs(1)
    d = o_ref.shape[1]

    @pl.when(t < nv_ref[0])
    def _():
        @pl.when(f == 0)
        def _():
            acc_s[...] = jnp.zeros_like(acc_s)

        hb = x_ref[:, :d]
        gt = _dot(hb, wg_ref[0])
        up = _dot(hb, wu_ref[0])
        act = (gt * jax.nn.sigmoid(gt) * up).astype(BF16)
        acc_s[...] += _dot(act, wd_ref[0])

        @pl.when(f == nf - 1)
        def _():
            pieces = x_ref[:, d:].astype(F32)
            lane = lax.broadcasted_iota(jnp.int32, pieces.shape, 1)
            mine = (lane < 3 * N_EXPERTS) & (lane % N_EXPERTS == te_ref[t])
            gate = jnp.sum(jnp.where(mine, pieces, 0.0), axis=-1, keepdims=True)
            o_ref[...] = (acc_s[...] * gate).astype(BF16)

    @pl.when((t >= nv_ref[0]) & (f == nf - 1))
    def _():
        o_ref[...] = jnp.zeros_like(o_ref)


def _grouped_ffn(xs, tab, wg, wu, wd, tm, tf):
    rows, w = xs.shape
    d = wg.shape[1]
    n_tiles = rows // tm
    nf = wg.shape[2] // tf
    live = lambda t, nv: jnp.minimum(t, nv[0] - 1)
    grid_spec = pltpu.PrefetchScalarGridSpec(
        num_scalar_prefetch=2,
        grid=(n_tiles, nf),
        in_specs=[pl.BlockSpec((tm, w), lambda t, f, te, nv: (live(t, nv), 0)),
                  pl.BlockSpec((1, d, tf), lambda t, f, te, nv: (te[t], 0, f)),
                  pl.BlockSpec((1, d, tf), lambda t, f, te, nv: (te[t], 0, f)),
                  pl.BlockSpec((1, tf, d), lambda t, f, te, nv: (te[t], f, 0))],
        out_specs=pl.BlockSpec((tm, d), lambda t, f, te, nv: (t, 0)),
        scratch_shapes=[pltpu.VMEM((tm, d), F32)])
    return pl.pallas_call(
        _gffn_kernel,
        out_shape=jax.ShapeDtypeStruct((rows, d), BF16),
        grid_spec=grid_spec,
        compiler_params=_params(("arbitrary", "arbitrary")),
        name="moe_grouped_ffn",
    )(tab['tile_e'], tab['n_live'], xs, wg, wu, wd)


def _combine_kernel(lo_ref, nc_ref, off_ref, info_ref, ys_hbm, x_ref, gate_ref, gp_ref, o_ref,
                    yall, sem, *, rmax):
    i = pl.program_id(0)
    n = pl.num_programs(0)
    slot = i % 2
    tb = x_ref.shape[1]

    def chunk_copy(blk, sl, e, c):
        k = blk * N_EXPERTS + e
        src = pl.multiple_of(off_ref[k] + c * MOE_SEG, MOE_SEG)
        dst = pl.multiple_of(lo_ref[k] + c * MOE_SEG, MOE_SEG)
        return pltpu.make_async_copy(ys_hbm.at[pl.ds(src, MOE_SEG)],
                                     yall.at[sl, pl.ds(dst, MOE_SEG)], sem.at[sl])

    def fetch(blk, sl):
        yall[sl, 2 * tb:, :] = jnp.zeros((rmax - 2 * tb, yall.shape[2]), BF16)
        _segment_loops(blk, nc_ref, lambda e, c: chunk_copy(blk, sl, e, c).start())

    @pl.when(i == 0)
    def _():
        fetch(0, 0)

    @pl.when(i + 1 < n)
    def _():
        fetch(i + 1, 1 - slot)

    _segment_loops(i, nc_ref, lambda e, c: chunk_copy(i, slot, e, c).wait())
    info = info_ref[0]
    loc1 = _local_rows(info[:, 0:1], info[:, 2:3], i, lo_ref)
    loc2 = _local_rows(info[:, 1:2], info[:, 3:4], i, lo_ref)
    lane = lax.broadcasted_iota(jnp.int32, (tb, rmax), 1).astype(F32)
    sel = jnp.where((lane == loc1) | (lane == loc2), 1.0, 0.0).astype(BF16)
    o_ref[0] = x_ref[0] + gate_ref[0] * _rms(_dot(sel, yall[slot]), gp_ref[...])


def _combine(ys, info, tab, x3, gate, gp, bpb, rmax):
    nblk, tb, d = x3.shape
    grid_spec = pltpu.PrefetchScalarGridSpec(
        num_scalar_prefetch=3,
        grid=(nblk,),
        in_specs=[pl.BlockSpec((1, tb, LANES), lambda i, *_: (i, 0, 0)),
                  pl.BlockSpec(memory_space=pl.ANY),
                  pl.BlockSpec((1, tb, d), lambda i, *_: (i, 0, 0)),
                  pl.BlockSpec((1, 1, d), lambda i, *_: (i // bpb, 0, 0)),
                  pl.BlockSpec((1, d), lambda i, *_: (0, 0))],
        out_specs=pl.BlockSpec((1, tb, d), lambda i, *_: (i, 0, 0)),
        scratch_shapes=[pltpu.VMEM((2, rmax, d), BF16), pltpu.SemaphoreType.DMA((2,))])
    return pl.pallas_call(
        functools.partial(_combine_kernel, rmax=rmax),
        out_shape=jax.ShapeDtypeStruct((nblk, tb, d), F32),
        grid_spec=grid_spec,
        compiler_params=_params(("arbitrary",)),
        name="moe_combine",
    )(tab['loc_off'], tab['nchunk'], tab['off'], info, ys, x3, gate, gp)


def _moe(x, sc, sh, gpre, router, wg, wu, wd, gate, gpost, tm, tf):
    b, l, d = x.shape
    tb = min(MOE_TB, l)
    bpb = l // tb
    nblk = b * bpb
    x3 = x.reshape(nblk, tb, d)
    hx, info, infot, cnt = _router(x3, sc, sh, gpre, router, bpb)
    cnt = cnt[:, 0, :N_EXPERTS].astype(jnp.int32)
    seg_pad = N_EXPERTS * (MOE_SEG - 1)
    rmax = -(-(2 * tb + seg_pad) // 256) * 256
    n_tiles = -(-(2 * b * l + nblk * seg_pad + N_EXPERTS * (tm - 1)) // tm)
    tab = _routing_tables(cnt, tm, n_tiles)
    xs = _dispatch(hx, infot, tab, n_tiles * tm, rmax)
    ys = _grouped_ffn(xs, tab, wg, wu, wd, tm, tf)
    return _combine(ys, info, tab, x3, gate, gpost, bpb, rmax).reshape(b, l, d)


def _rope_tables(s):
    half = DIFF_DK // 2
    n_freq = half // 2
    inv = ROPE_THETA ** (-jnp.arange(n_freq, dtype=F32) / n_freq)
    t = jnp.arange(s, dtype=jnp.int32)
    rows = (t // GRID_W).astype(F32)[:, None]
    cols = (t % GRID_W).astype(F32)[:, None]
    lane = jnp.arange(LANES, dtype=jnp.int32)
    dd = lane % DIFF_DK
    pos = jnp.where((dd < half)[None, :], rows, cols)
    ang = pos * inv[dd % n_freq][None, :]
    first = ((dd % half) < n_freq)[None, :]
    cos, sin = jnp.cos(ang), jnp.sin(ang)
    return cos, jnp.where(first, -sin, 0.0), jnp.where(first, 0.0, sin)


def _dft_tables(n):
    j = jnp.arange(n, dtype=jnp.int32)
    idx = (j[:, None] * j[None, :]) % n
    ang = idx.astype(F32) * (2.0 * math.pi / n)
    sc = n ** -0.5
    return jnp.cos(ang) * sc, jnp.sin(ang) * sc


def _channel_dft():
    c, s = _dft_tables(FNET_GROUP_CH)
    eye = jnp.eye(FNET_GROUPS, dtype=F32)
    return jnp.concatenate([jnp.kron(eye, c), jnp.kron(eye, s)], axis=1).astype(BF16)


def kernel(x, c, ctx, c_ctx, ada_w, ada_b, norm_mix_pre, norm_mix_post, norm_ffn_pre, norm_ffn_post,
           w_in, w_out, diff_lq1, diff_lk1, diff_lq2, diff_lk2, diff_subln, fnet_w,
           s5_a_re, s5_a_im, s5_log_dt, s5_b_re, s5_b_im, s5_c_re, s5_c_im, s5_d, s5_w_glu,
           ffn_w_gate, ffn_w_up, ffn_w_down, moe_router, moe_w_gate, moe_w_up, moe_w_down):
    b, s, d = x.shape
    lc = ctx.shape[1]
    depth = ada_w.shape[0]

    cc = jnp.zeros((MOD_ROWS, d), F32).at[:b].set(c).at[b].set(c_ctx)
    mod = _modulation(cc, ada_w, ada_b)

    rope_tabs = _rope_tables(s)
    cs64 = _channel_dft()
    dft = {n: tuple(t.astype(BF16) for t in _dft_tables(n)) for n in (s, lc)}

    xc = ctx
    for l in range(depth):
        need_ctx = l < depth - 1
        lam_init = 0.8 - 0.6 * math.exp(-0.3 * l)
        m_lat = [mod[l, :b, i * d:(i + 1) * d].reshape(b, 1, d) for i in range(6)]
        m_ctx = [jnp.broadcast_to(mod[l, b, i * d:(i + 1) * d].reshape(1, 1, d), (b, 1, d))
                 for i in range(6)]
        row = lambda v: v.reshape(1, -1).astype(F32)
        w_in_bf = w_in[l].astype(BF16)
        w_out_bf = w_out[l].astype(BF16)
        fw_bf = fnet_w[l].astype(BF16)
        wglu_bf = s5_w_glu[l].astype(BF16)
        lams = (row(diff_lq1[l]), row(diff_lk1[l]), row(diff_lq2[l]), row(diff_lk2[l]))
        subln = row(diff_subln[l])

        q, k, v, g1, g2, u = _inproj(x, m_lat[1], m_lat[0], row(norm_mix_pre[l]), w_in_bf, cs64,
                                     rope_tabs, min(INPROJ_TL, s))
        qc, kc, vc, g1c, g2c, uc = _inproj(xc, m_ctx[1], m_ctx[0], row(norm_mix_pre[l]), w_in_bf,
                                           cs64, None, lc)
        a_lat = _attention(q, [(kc, vc), (k, v)], lams, subln, lam_init, min(ATTN_TQ, s))
        f_lat = _fnet_dft(*dft[s], g1, g2, fw_bf, b, min(512, s), 2)

        mats = [_s5_matrices(s5_a_re[l, dr], s5_a_im[l, dr], s5_log_dt[l, dr], s5_b_re[l, dr],
                             s5_b_im[l, dr], s5_c_re[l, dr], s5_c_im[l, dr]) for dr in (0, 1)]
        mats = (mats[0][0], mats[1][0], mats[0][1], mats[1][1], mats[0][2], mats[1][2])
        zero_state = jnp.zeros((b, 2 * S5_LANES), F32)
        ycf, ycr, hcf, hcr = _s5_scan(uc, mats, zero_state, zero_state, b, S5_TC)
        yf, yr, _, _ = _s5_scan(u, mats, hcf, hcr, b, S5_TC)

        x = _outproj(a_lat, f_lat, yf, yr, u, row(s5_d[l]), wglu_bf, w_out_bf,
                     x, m_lat[2], row(norm_mix_post[l]), min(512, s))
        if need_ctx:
            a_ctx = _attention(qc, [(kc, vc)], lams, subln, lam_init, lc)
            f_ctx = _fnet_dft(*dft[lc], g1c, g2c, fw_bf, b, lc, 2)
            xc = _outproj(a_ctx, f_ctx, ycf, ycr, uc, row(s5_d[l]), wglu_bf,
                          w_out_bf, xc, m_ctx[2], row(norm_mix_post[l]), lc)

        i = l // 2
        gpre, gpost = row(norm_ffn_pre[l]), row(norm_ffn_post[l])
        xc1 = xc.reshape(1, b * lc, d)
        if l % 2 == 0:
            wg, wu, wd = (ffn_w_gate[i].astype(BF16), ffn_w_up[i].astype(BF16),
                          ffn_w_down[i].astype(BF16))
            tf = FFN_DENSE_TF
            x = _ffn(x, m_lat[4], m_lat[3], gpre, wg, wu, wd, m_lat[5], gpost, min(1024, s), tf)
            if need_ctx:
                xc1 = _ffn(xc1, m_ctx[4][:1], m_ctx[3][:1], gpre, wg, wu, wd, m_ctx[5][:1], gpost,
                           min(1024, b * lc), tf)
        else:
            wg, wu, wd = (moe_w_gate[i].astype(BF16), moe_w_up[i].astype(BF16),
                          moe_w_down[i].astype(BF16))
            r = jnp.zeros((d, LANES), F32).at[:, :N_EXPERTS].set(moe_router[i])
            r_hi = r.astype(BF16)
            router = jnp.stack([r_hi, (r - r_hi.astype(F32)).astype(BF16)])
            x = _moe(x, m_lat[4], m_lat[3], gpre, router, wg, wu, wd, m_lat[5], gpost,
                     MOE_TM, MOE_TF)
            if need_ctx:
                xc1 = _moe(xc1, m_ctx[4][:1], m_ctx[3][:1], gpre, router, wg, wu, wd, m_ctx[5][:1],
                           gpost, MOE_TM, MOE_TF)
        xc = xc1.reshape(b, lc, d)
    return x
```

```python
import functools
import math

import jax
import jax.numpy as jnp
from jax import lax
from jax.experimental import pallas as pl
from jax.experimental.pallas import tpu as pltpu

F32 = jnp.float32
BF16 = jnp.bfloat16

D_MODEL = 1024
DEPTH = 2
GRID_W = 64
EPS = 1e-6
DIFF_HEADS = 4
DIFF_DK = 64
DIFF_DV = 2 * DIFF_DK
DIFF_WIDTH = DIFF_HEADS * DIFF_DV
DIFF_QK_WIDTH = DIFF_HEADS * 2 * DIFF_DK
ROPE_THETA = 10000.0
FNET_GROUPS = 4
FNET_GROUP_CH = 64
FNET_WIDTH = FNET_GROUPS * FNET_GROUP_CH
S5_CH = 16
S5_GROUPS = 16
S5_STATE = 64
S5_WIDTH = S5_GROUPS * S5_CH
S5_LANES = S5_GROUPS * S5_STATE
IN_WIDTH = 2 * DIFF_QK_WIDTH + DIFF_WIDTH + FNET_WIDTH + S5_WIDTH
N_EXPERTS = 8
LANES = 128
LOG2E = math.log2(math.e)
MOD_ROWS = 24
FFN_DENSE_TF = 1408
INPROJ_TL = 1024
INPROJ_RSUB = 256
ATTN_TQ = 2048
ATTN_RSUB = 128
S5_TC = 32
MOE_TB = 1024
MOE_SEG = 32
MOE_RSUB = 768
MOE_CSUB = 256
MOE_TM = 1024
MOE_TF = 1792
MOE_FSUB = 896

VMEM_LIMIT = 56 * 1024 * 1024


def _params(sem):
    return pltpu.CompilerParams(dimension_semantics=sem, vmem_limit_bytes=VMEM_LIMIT)


def _rms(x, g):
    return x * lax.rsqrt(jnp.mean(x * x, axis=-1, keepdims=True) + EPS) * g


def _dot(a, b):
    return jnp.dot(a, b, preferred_element_type=F32)


def _mod_kernel(c_ref, w_ref, b_ref, o_ref):
    c = c_ref[...]
    sc = c * jax.nn.sigmoid(c)
    o_ref[0] = jnp.dot(sc, w_ref[0], preferred_element_type=F32,
                       precision=lax.Precision.HIGHEST) + b_ref[0]


def _modulation(cc, ada_w, ada_b):
    depth, d, n = ada_w.shape
    tn = 1536
    return pl.pallas_call(
        _mod_kernel,
        out_shape=jax.ShapeDtypeStruct((depth, MOD_ROWS, n), F32),
        grid=(depth, n // tn),
        in_specs=[pl.BlockSpec((MOD_ROWS, d), lambda l, j: (0, 0)),
                  pl.BlockSpec((1, d, tn), lambda l, j: (l, 0, j)),
                  pl.BlockSpec((1, 1, tn), lambda l, j: (l, 0, j))],
        out_specs=pl.BlockSpec((1, MOD_ROWS, tn), lambda l, j: (l, 0, j)),
        compiler_params=_params(("parallel", "parallel")),
        name="adaln_mod",
    )(cc, ada_w, ada_b.reshape(depth, 1, n))


def _inproj_kernel(*refs, rope):
    if rope:
        (x_ref, sc_ref, sh_ref, g_ref, w_ref, cs_ref, cos_ref, sina_ref, sinb_ref,
         q_ref, k_ref, v_ref, g1_ref, g2_ref, u_ref) = refs
    else:
        (x_ref, sc_ref, sh_ref, g_ref, w_ref, cs_ref,
         q_ref, k_ref, v_ref, g1_ref, g2_ref, u_ref) = refs
    tl = x_ref.shape[1]
    rsub = min(INPROJ_RSUB, tl)
    for r0 in range(0, tl, rsub):
        rows = slice(r0, r0 + rsub)
        h = _rms(x_ref[0, rows, :], g_ref[...]) * (1.0 + sc_ref[0]) + sh_ref[0]
        hb = h.astype(BF16)

        def proj(lo, hi):
            return _dot(hb, w_ref[:, lo:hi])

        def rotate(t):
            outs = []
            for j in range(t.shape[1] // LANES):
                tb = t[:, j * LANES:(j + 1) * LANES]
                outs.append(tb * cos_ref[rows, :]
                            + pltpu.roll(tb, LANES - 16, axis=1) * sina_ref[rows, :]
                            + pltpu.roll(tb, 16, axis=1) * sinb_ref[rows, :])
            return jnp.concatenate(outs, axis=1)

        q = proj(0, DIFF_QK_WIDTH)
        k = proj(DIFF_QK_WIDTH, 2 * DIFF_QK_WIDTH)
        if rope:
            q = rotate(q)
            k = rotate(k)
        q_ref[0, rows, :] = (q * (DIFF_DK ** -0.5 * LOG2E)).astype(BF16)
        k_ref[0, rows, :] = k.astype(BF16)
        o = 2 * DIFF_QK_WIDTH
        v_ref[0, rows, :] = proj(o, o + DIFF_WIDTH).astype(BF16)
        o += DIFF_WIDTH
        f = proj(o, o + FNET_WIDTH).astype(BF16)
        g12 = _dot(f, cs_ref[...])
        g1_ref[rows, :] = g12[:, :FNET_WIDTH].astype(BF16)
        g2_ref[rows, :] = g12[:, FNET_WIDTH:].astype(BF16)
        u_ref[rows, :] = proj(o + FNET_WIDTH, IN_WIDTH)


def _inproj(x, sc, sh, g, w_bf, cs64, rope_tabs, tl):
    b, l, d = x.shape
    rope = rope_tabs is not None
    in_specs = [pl.BlockSpec((1, tl, d), lambda i, bb: (bb, i, 0)),
                pl.BlockSpec((1, 1, d), lambda i, bb: (bb, 0, 0)),
                pl.BlockSpec((1, 1, d), lambda i, bb: (bb, 0, 0)),
                pl.BlockSpec((1, d), lambda i, bb: (0, 0)),
                pl.BlockSpec((d, IN_WIDTH), lambda i, bb: (0, 0)),
                pl.BlockSpec((FNET_WIDTH, 2 * FNET_WIDTH), lambda i, bb: (0, 0))]
    args = [x, sc, sh, g, w_bf, cs64]
    if rope:
        in_specs += [pl.BlockSpec((tl, LANES), lambda i, bb: (i, 0))] * 3
        args += list(rope_tabs)
    tok = lambda w: pl.BlockSpec((1, tl, w), lambda i, bb: (bb, i, 0))
    tb = pl.BlockSpec((tl, FNET_WIDTH), lambda i, bb: (i, bb))
    return pl.pallas_call(
        functools.partial(_inproj_kernel, rope=rope),
        out_shape=(jax.ShapeDtypeStruct((b, l, DIFF_QK_WIDTH), BF16),
                   jax.ShapeDtypeStruct((b, l, DIFF_QK_WIDTH), BF16),
                   jax.ShapeDtypeStruct((b, l, DIFF_WIDTH), BF16),
                   jax.ShapeDtypeStruct((l, b * FNET_WIDTH), BF16),
                   jax.ShapeDtypeStruct((l, b * FNET_WIDTH), BF16),
                   jax.ShapeDtypeStruct((l, b * S5_WIDTH), F32)),
        grid=(l // tl, b),
        in_specs=in_specs,
        out_specs=(tok(DIFF_QK_WIDTH), tok(DIFF_QK_WIDTH), tok(DIFF_WIDTH), tb, tb, tb),
        compiler_params=_params(("parallel", "parallel")),
        name="inproj_rope" if rope else "inproj",
    )(*args)


def _attn_kernel(*refs, n_src, lam_init, rsub):
    q_ref = refs[0]
    kv = refs[1:1 + 2 * n_src]
    lq1, lk1, lq2, lk2, sub_ref, o_ref, k1_s, k2_s, v_s = refs[1 + 2 * n_src:]

    @pl.when(pl.program_id(2) == 0)
    def _():
        off = 0
        for s in range(n_src):
            kk = kv[2 * s][0]
            n = kk.shape[0]
            k1_s[off:off + n, :] = kk[:, :DIFF_DK]
            k2_s[off:off + n, :] = kk[:, DIFF_DK:]
            v_s[off:off + n, :DIFF_DV] = kv[2 * s + 1][0]
            off += n
        v_s[:, DIFF_DV:] = jnp.ones((v_s.shape[0], DIFF_DV), BF16)

    lam = (jnp.exp(jnp.sum(lq1[...] * lk1[...], axis=-1, keepdims=True))
           - jnp.exp(jnp.sum(lq2[...] * lk2[...], axis=-1, keepdims=True)) + lam_init)
    def attend(qj, k_s):
        s = lax.dot_general(qj, k_s[...], (((1,), (1,)), ((), ())), preferred_element_type=F32)
        p = jnp.exp2(s - jnp.max(s, axis=-1, keepdims=True)).astype(BF16)
        ol = _dot(p, v_s[...])
        return ol[:, :DIFF_DV] / ol[:, DIFF_DV:DIFF_DV + 1]

    for r0 in range(0, q_ref.shape[1], rsub):
        q = q_ref[0, r0:r0 + rsub, :]
        o = attend(q[:, :DIFF_DK], k1_s) - lam * attend(q[:, DIFF_DK:], k2_s)
        o_ref[0, r0:r0 + rsub, :] = (_rms(o, sub_ref[...]) * (1.0 - lam_init)).astype(BF16)


def _attention(q, kv_srcs, lams, subln, lam_init, tq):
    b, lq, _ = q.shape
    n_src = len(kv_srcs)
    lk = sum(k.shape[1] for k, _ in kv_srcs)
    in_specs = [pl.BlockSpec((1, tq, DIFF_DV), lambda bb, h, i: (bb, i, h))]
    args = [q]
    for k, v in kv_srcs:
        spec = pl.BlockSpec((1, k.shape[1], DIFF_DV), lambda bb, h, i: (bb, 0, h))
        in_specs += [spec, spec]
        args += [k, v]
    in_specs += [pl.BlockSpec((1, DIFF_DK), lambda bb, h, i: (0, 0))] * 4
    in_specs += [pl.BlockSpec((1, DIFF_DV), lambda bb, h, i: (0, 0))]
    args += list(lams) + [subln]
    return pl.pallas_call(
        functools.partial(_attn_kernel, n_src=n_src, lam_init=lam_init, rsub=min(ATTN_RSUB, tq)),
        out_shape=jax.ShapeDtypeStruct((b, lq, DIFF_WIDTH), BF16),
        grid=(b, DIFF_HEADS, lq // tq),
        in_specs=in_specs,
        out_specs=pl.BlockSpec((1, tq, DIFF_DV), lambda bb, h, i: (bb, i, h)),
        scratch_shapes=[pltpu.VMEM((lk, DIFF_DK), BF16), pltpu.VMEM((lk, DIFF_DK), BF16),
                        pltpu.VMEM((lk, 2 * DIFF_DV), BF16)],
        compiler_params=_params(("parallel", "parallel", "arbitrary")),
        name="diff_attn_%d" % n_src,
    )(*args)


def _dft_kernel(c_ref, s_ref, g1_ref, g2_ref, w_ref, o_ref):
    z = _dot(c_ref[...], g1_ref[...]) - _dot(s_ref[...], g2_ref[...])
    for j in range(o_ref.shape[0]):
        zj = z[:, j * FNET_WIDTH:(j + 1) * FNET_WIDTH].astype(BF16)
        o_ref[j] = _dot(zj, w_ref[...]).astype(BF16)


def _fnet_dft(cosm, sinm, g1, g2, w_bf, b, tm, nb):
    l = cosm.shape[0]
    tn = nb * FNET_WIDTH
    return pl.pallas_call(
        _dft_kernel,
        out_shape=jax.ShapeDtypeStruct((b, l, FNET_WIDTH), BF16),
        grid=(l // tm, b // nb),
        in_specs=[pl.BlockSpec((tm, l), lambda i, j: (i, 0)),
                  pl.BlockSpec((tm, l), lambda i, j: (i, 0)),
                  pl.BlockSpec((l, tn), lambda i, j: (0, j)),
                  pl.BlockSpec((l, tn), lambda i, j: (0, j)),
                  pl.BlockSpec((FNET_WIDTH, FNET_WIDTH), lambda i, j: (0, 0))],
        out_specs=pl.BlockSpec((nb, tm, FNET_WIDTH), lambda i, j: (j, i, 0)),
        compiler_params=_params(("parallel", "parallel")),
        name="fnet_dft",
    )(cosm, sinm, g1, g2, w_bf)


def _s5_kernel(uf_ref, ur_ref, perm_ref, permt_ref, bf_ref, br_ref, lf_ref, lr_ref, cf_ref, cr_ref,
               h0f_ref, h0r_ref, yf_ref, yr_ref, hef_ref, her_ref, hs_f, hs_r, hb_f, hb_r,
               *, tc, nb):
    j = pl.program_id(0)

    @pl.when(j == 0)
    def _():
        hs_f[...] = h0f_ref[...]
        hs_r[...] = h0r_ref[...]

    def drive(u_ref, b_ref):
        u_bt = jnp.concatenate(
            [u_ref[:, bb * S5_WIDTH:(bb + 1) * S5_WIDTH] for bb in range(nb)], axis=0).astype(BF16)
        u_tb = _dot(perm_ref[...], u_bt).astype(BF16)
        return _dot(u_tb, b_ref[...])

    def scan(drv, l_ref, hs, hb, reverse):
        hr, hi = hs[:, :S5_LANES], hs[:, S5_LANES:]
        for t in (range(tc - 1, -1, -1) if reverse else range(tc)):
            lre = l_ref[:, :S5_LANES]
            lim = l_ref[:, S5_LANES:]
            d = drv[t * nb:(t + 1) * nb, :]
            hr, hi = (lre * hr - lim * hi + d[:, :S5_LANES], lre * hi + lim * hr + d[:, S5_LANES:])
            hb[t * nb:(t + 1) * nb, :S5_LANES] = hr.astype(BF16)
            hb[t * nb:(t + 1) * nb, S5_LANES:] = hi.astype(BF16)
        hs[:, :S5_LANES] = hr
        hs[:, S5_LANES:] = hi

    def readout(hb, c_ref, y_ref):
        y = _dot(hb[...], c_ref[...])
        y_hi = y.astype(BF16)
        y_lo = (y - y_hi.astype(F32)).astype(BF16)
        y_bt = _dot(permt_ref[...], y_hi) + _dot(permt_ref[...], y_lo)
        for bb in range(nb):
            y_ref[:, bb * S5_WIDTH:(bb + 1) * S5_WIDTH] = y_bt[bb * tc:(bb + 1) * tc, :]

    drv_f = drive(uf_ref, bf_ref)
    drv_r = drive(ur_ref, br_ref)
    scan(drv_f, lf_ref, hs_f, hb_f, False)
    scan(drv_r, lr_ref, hs_r, hb_r, True)
    readout(hb_f, cf_ref, yf_ref)
    readout(hb_r, cr_ref, yr_ref)

    @pl.when(j == pl.num_programs(0) - 1)
    def _():
        hef_ref[...] = hs_f[...]
        her_ref[...] = hs_r[...]


def _s5_scan(u, mats, h0f, h0r, nb, tc):
    l = u.shape[0]
    n = l // tc
    r = tc * nb
    bmf, bmr, lf, lr, cmf, cmr = mats
    lf, lr = (jnp.broadcast_to(v, (nb, v.shape[1])) for v in (lf, lr))
    rows = jnp.arange(r, dtype=jnp.int32)
    perm = (rows[None, :] == ((rows % nb) * tc + rows // nb)[:, None]).astype(BF16)
    permt = perm.T
    full = lambda a: pl.BlockSpec(a.shape, lambda j: (0,) * a.ndim)
    fwd = pl.BlockSpec((tc, nb * S5_WIDTH), lambda j: (j, 0))
    rev = pl.BlockSpec((tc, nb * S5_WIDTH), lambda j: (n - 1 - j, 0))
    st = jax.ShapeDtypeStruct((nb, 2 * S5_LANES), F32)
    return pl.pallas_call(
        functools.partial(_s5_kernel, tc=tc, nb=nb),
        out_shape=(jax.ShapeDtypeStruct(u.shape, F32), jax.ShapeDtypeStruct(u.shape, F32), st, st),
        grid=(n,),
        in_specs=[fwd, rev, full(perm), full(permt), full(bmf), full(bmr), full(lf), full(lr),
                  full(cmf), full(cmr), full(h0f), full(h0r)],
        out_specs=(fwd, rev, full(h0f), full(h0r)),
        scratch_shapes=[pltpu.VMEM((nb, 2 * S5_LANES), F32), pltpu.VMEM((nb, 2 * S5_LANES), F32),
                        pltpu.VMEM((r, 2 * S5_LANES), BF16), pltpu.VMEM((r, 2 * S5_LANES), BF16)],
        compiler_params=_params(("arbitrary",)),
        name="s5_scan",
    )(u, u, perm, permt, bmf, bmr, lf, lr, cmf, cmr, h0f, h0r)


def _s5_matrices(a_re, a_im, log_dt, b_re, b_im, c_re, c_im):
    dt = jnp.exp(log_dt)[:, None]
    mag = jnp.exp(a_re * dt)
    lr, li = mag * jnp.cos(a_im * dt), mag * jnp.sin(a_im * dt)
    nr, ni = lr - 1.0, li
    den = a_re * a_re + a_im * a_im
    cr = (nr * a_re + ni * a_im) / den
    ci = (ni * a_re - nr * a_im) / den
    bbr = cr[..., None] * b_re - ci[..., None] * b_im
    bbi = cr[..., None] * b_im + ci[..., None] * b_re
    eye = jnp.eye(S5_GROUPS, dtype=F32)

    def drive_mat(bb):
        return jnp.einsum('gpc,gh->gchp', bb, eye).reshape(S5_WIDTH, S5_LANES)

    def read_mat(cc):
        return jnp.einsum('gcp,gh->gphc', cc, eye).reshape(S5_LANES, S5_WIDTH)

    bm = jnp.concatenate([drive_mat(bbr), drive_mat(bbi)], axis=1).astype(BF16)
    cm = jnp.concatenate([read_mat(c_re), -read_mat(c_im)], axis=0).astype(BF16)
    lam = jnp.concatenate([lr.reshape(1, S5_LANES), li.reshape(1, S5_LANES)], axis=1)
    return bm, lam, cm


def _outproj_kernel(a_ref, f_ref, yf_ref, yr_ref, u_ref, d_ref, wglu_ref, wa_ref, wf_ref, ws_ref,
                    x_ref, gate_ref, g_ref, o_ref):
    yy = yf_ref[...] + yr_ref[...] + d_ref[...] * u_ref[...]
    yy = jax.nn.gelu(yy)
    s = yy * jax.nn.sigmoid(_dot(yy.astype(BF16), wglu_ref[...]))
    mix = (_dot(a_ref[0], wa_ref[...]) + _dot(f_ref[0], wf_ref[...])
           + _dot(s.astype(BF16), ws_ref[...]))
    o_ref[0] = x_ref[0] + gate_ref[0] * _rms(mix, g_ref[...])


def _outproj(a, fo, yf, yr, u, d, wglu, w_out, x, gate, g, tl):
    b, l, dm = x.shape
    wa, wf, ws = (w_out[:DIFF_WIDTH], w_out[DIFF_WIDTH:DIFF_WIDTH + FNET_WIDTH],
                  w_out[DIFF_WIDTH + FNET_WIDTH:])
    tok = lambda w: pl.BlockSpec((1, tl, w), lambda i, bb: (bb, i, 0))
    tb = pl.BlockSpec((tl, S5_WIDTH), lambda i, bb: (i, bb))
    full = lambda arr: pl.BlockSpec(arr.shape, lambda i, bb: (0,) * arr.ndim)
    return pl.pallas_call(
        _outproj_kernel,
        out_shape=jax.ShapeDtypeStruct((b, l, dm), F32),
        grid=(l // tl, b),
        in_specs=[tok(DIFF_WIDTH), tok(FNET_WIDTH), tb, tb, tb, full(d), full(wglu),
                  full(wa), full(wf), full(ws), tok(dm),
                  pl.BlockSpec((1, 1, dm), lambda i, bb: (bb, 0, 0)), full(g)],
        out_specs=tok(dm),
        compiler_params=_params(("parallel", "parallel")),
        name="outproj",
    )(a, fo, yf, yr, u, d, wglu, wa, wf, ws, x, gate, g)


def _ffn_kernel(x_ref, sc_ref, sh_ref, g_ref, wg_ref, wu_ref, wd_ref, gate_ref, gp_ref,
                o_ref, h_s, acc_s):
    f = pl.program_id(2)

    @pl.when(f == 0)
    def _():
        h = _rms(x_ref[0], g_ref[...]) * (1.0 + sc_ref[0]) + sh_ref[0]
        h_s[...] = h.astype(BF16)
        acc_s[...] = jnp.zeros_like(acc_s)

    hb = h_s[...]
    gt = _dot(hb, wg_ref[...])
    up = _dot(hb, wu_ref[...])
    act = (gt * jax.nn.sigmoid(gt) * up).astype(BF16)
    acc_s[...] += _dot(act, wd_ref[...])

    @pl.when(f == pl.num_programs(2) - 1)
    def _():
        o_ref[0] = x_ref[0] + gate_ref[0] * _rms(acc_s[...], gp_ref[...])


def _ffn(x, sc, sh, g, wg, wu, wd, gate, gp, tm, tf):
    b, l, d = x.shape
    fdim = wg.shape[1]
    tok = pl.BlockSpec((1, tm, d), lambda bb, i, f: (bb, i, 0))
    modv = pl.BlockSpec((1, 1, d), lambda bb, i, f: (bb, 0, 0))
    vec = pl.BlockSpec((1, d), lambda bb, i, f: (0, 0))
    return pl.pallas_call(
        _ffn_kernel,
        out_shape=jax.ShapeDtypeStruct((b, l, d), F32),
        grid=(b, l // tm, fdim // tf),
        in_specs=[tok, modv, modv, vec,
                  pl.BlockSpec((d, tf), lambda bb, i, f: (0, f)),
                  pl.BlockSpec((d, tf), lambda bb, i, f: (0, f)),
                  pl.BlockSpec((tf, d), lambda bb, i, f: (f, 0)),
                  modv, vec],
        out_specs=tok,
        scratch_shapes=[pltpu.VMEM((tm, d), BF16), pltpu.VMEM((tm, d), F32)],
        compiler_params=_params(("parallel", "parallel", "arbitrary")),
        name="dense_ffn",
    )(x, sc, sh, g, wg, wu, wd, gate, gp)


def _router_kernel(x_ref, sc_ref, sh_ref, g_ref, r_ref, tri_ref, hx_ref, info_ref, infot_ref, cnt_ref):
    d = x_ref.shape[2]
    h = _rms(x_ref[0], g_ref[...]) * (1.0 + sc_ref[0]) + sh_ref[0]
    hi = h.astype(BF16)
    lo = (h - hi.astype(F32)).astype(BF16)
    logits = _dot(hi, r_ref[0]) + _dot(hi, r_ref[1]) + _dot(lo, r_ref[0])
    lane = lax.broadcasted_iota(jnp.int32, logits.shape, 1).astype(F32)
    neg = jnp.float32(-jnp.inf)
    logits = jnp.where(lane < N_EXPERTS, logits, neg)
    m1 = jnp.max(logits, axis=-1, keepdims=True)
    i1 = jnp.min(jnp.where(logits == m1, lane, float(LANES)), axis=-1, keepdims=True)
    rest = jnp.where(lane == i1, neg, logits)
    m2 = jnp.max(rest, axis=-1, keepdims=True)
    i2 = jnp.min(jnp.where(rest == m2, lane, float(LANES)), axis=-1, keepdims=True)
    e2 = jnp.exp(m2 - m1)
    w1 = 1.0 / (1.0 + e2)
    oh1 = lane == i1
    oh2 = lane == i2
    member = jnp.where(oh1 | oh2, 1.0, 0.0)
    ranks = _dot(tri_ref[...], member.astype(BF16))
    pos1 = jnp.sum(jnp.where(oh1, ranks, 0.0), axis=-1, keepdims=True)
    pos2 = jnp.sum(jnp.where(oh2, ranks, 0.0), axis=-1, keepdims=True)
    cnt_ref[0] = jnp.broadcast_to(jnp.sum(member, axis=0, keepdims=True), cnt_ref.shape[1:])
    gates = jnp.where(oh1, w1, 0.0) + jnp.where(oh2, e2 * w1, 0.0)
    g_hi = gates.astype(BF16).astype(F32)
    g_mid = (gates - g_hi).astype(BF16).astype(F32)
    g_lo = gates - g_hi - g_mid
    pieces = g_hi + pltpu.roll(g_mid, N_EXPERTS, axis=1) + pltpu.roll(g_lo, 2 * N_EXPERTS, axis=1)
    hx_ref[0, :, :d] = hi
    hx_ref[0, :, d:] = pieces.astype(BF16)
    info = jnp.where(lane == 0, i1, jnp.where(lane == 1, i2, jnp.where(
        lane == 2, pos1, jnp.where(lane == 3, pos2, 0.0))))
    info_ref[0] = info
    infot_ref[0] = info.T[:8, :]


def _router(x3, sc, sh, g, router, bpb):
    nblk, tb, d = x3.shape
    tok = lambda w: pl.BlockSpec((1, tb, w), lambda i: (i, 0, 0))
    modv = pl.BlockSpec((1, 1, d), lambda i: (i // bpb, 0, 0))
    r = jnp.arange(tb, dtype=jnp.int32)
    tri = (r[None, :] < r[:, None]).astype(BF16)
    return pl.pallas_call(
        _router_kernel,
        out_shape=(jax.ShapeDtypeStruct((nblk, tb, d + LANES), BF16),
                   jax.ShapeDtypeStruct((nblk, tb, LANES), F32),
                   jax.ShapeDtypeStruct((nblk, 8, tb), F32),
                   jax.ShapeDtypeStruct((nblk, 8, LANES), F32)),
        grid=(nblk,),
        in_specs=[tok(d), modv, modv, pl.BlockSpec((1, d), lambda i: (0, 0)),
                  pl.BlockSpec(router.shape, lambda i: (0, 0, 0)),
                  pl.BlockSpec((tb, tb), lambda i: (0, 0))],
        out_specs=(tok(d + LANES), tok(LANES), pl.BlockSpec((1, 8, tb), lambda i: (i, 0, 0)),
                   pl.BlockSpec((1, 8, LANES), lambda i: (i, 0, 0))),
        compiler_params=_params(("parallel",)),
        name="moe_router",
    )(x3, sc, sh, g, router, tri)


def _routing_tables(cnt, tm, n_tiles):
    i32 = jnp.int32
    pcnt = ((cnt + MOE_SEG - 1) // MOE_SEG) * MOE_SEG
    loc_off = jnp.cumsum(pcnt, axis=1) - pcnt
    used = jnp.sum(pcnt, axis=0)
    region = ((used + tm - 1) // tm) * tm
    ends = jnp.cumsum(region)
    base = ends - region
    off = base[None, :] + jnp.cumsum(pcnt, axis=0) - pcnt
    tile_row0 = jnp.arange(n_tiles, dtype=i32) * tm
    tile_e = jnp.minimum(jnp.sum((ends[None, :] <= tile_row0[:, None]).astype(i32), axis=1),
                         N_EXPERTS - 1)
    flat = lambda a: a.reshape(-1).astype(i32)
    zstart = jnp.concatenate([base + used, ends[-1:]])
    zrows = jnp.concatenate([region - used, n_tiles * tm - ends[-1:]])
    return dict(loc_off=flat(loc_off), nchunk=flat(pcnt // MOE_SEG), off=flat(off),
                zstart=flat(zstart), znchunk=flat(zrows // MOE_SEG),
                tile_e=tile_e.astype(i32), n_live=(ends[-1:] // tm).astype(i32))


def _segment_loops(blk, nchunk_ref, fn):
    for e in range(N_EXPERTS):
        def body(c, carry, e=e):
            fn(e, c)
            return carry
        lax.fori_loop(0, nchunk_ref[blk * N_EXPERTS + e], body, 0)


def _local_rows(idx, pos, blk, lo_ref):
    loc = pos
    for e in range(N_EXPERTS):
        loc = loc + jnp.where(idx == float(e), lo_ref[blk * N_EXPERTS + e].astype(F32), 0.0)
    return loc


def _dispatch_kernel(lo_ref, nc_ref, off_ref, zs_ref, zn_ref, hx_ref, it_ref, xs_hbm,
                     xall, zbuf, sem, zsem, *, rmax):
    i = pl.program_id(0)
    n = pl.num_programs(0)
    slot = i % 2

    def chunk_copy(blk, sl, e, c):
        k = blk * N_EXPERTS + e
        src = pl.multiple_of(lo_ref[k] + c * MOE_SEG, MOE_SEG)
        dst = pl.multiple_of(off_ref[k] + c * MOE_SEG, MOE_SEG)
        return pltpu.make_async_copy(xall.at[sl, pl.ds(src, MOE_SEG)],
                                     xs_hbm.at[pl.ds(dst, MOE_SEG)], sem.at[sl])

    def zero_copy(e, c):
        dst = pl.multiple_of(zs_ref[e] + c * MOE_SEG, MOE_SEG)
        return pltpu.make_async_copy(zbuf, xs_hbm.at[pl.ds(dst, MOE_SEG)], zsem.at[0])

    @pl.when(i >= 2)
    def _():
        _segment_loops(i - 2, nc_ref, lambda e, c: chunk_copy(i - 2, slot, e, c).wait())

    it = it_ref[0]
    loc1 = _local_rows(it[0:1, :], it[2:3, :], i, lo_ref)
    loc2 = _local_rows(it[1:2, :], it[3:4, :], i, lo_ref)
    hx = hx_ref[0]
    rsub = MOE_RSUB if rmax % MOE_RSUB == 0 else rmax
    for r0 in range(0, rmax, rsub):
        row = (lax.broadcasted_iota(jnp.int32, (rsub, it.shape[1]), 0) + r0).astype(F32)
        sel = jnp.where((row == loc1) | (row == loc2), 1.0, 0.0).astype(BF16)
        xall[slot, r0:r0 + rsub, :] = _dot(sel, hx).astype(BF16)
    _segment_loops(i, nc_ref, lambda e, c: chunk_copy(i, slot, e, c).start())

    @pl.when(i == n - 1)
    def _():
        zbuf[...] = jnp.zeros_like(zbuf)
        for z in range(N_EXPERTS + 1):
            def zbody(c, carry, z=z):
                zero_copy(z, c).start()
                return carry
            lax.fori_loop(0, zn_ref[z], zbody, 0)

        @pl.when(n >= 2)
        def _():
            _segment_loops(i - 1, nc_ref, lambda e, c: chunk_copy(i - 1, 1 - slot, e, c).wait())
        _segment_loops(i, nc_ref, lambda e, c: chunk_copy(i, slot, e, c).wait())
        for z in range(N_EXPERTS + 1):
            def wbody(c, carry, z=z):
                zero_copy(z, c).wait()
                return carry
            lax.fori_loop(0, zn_ref[z], wbody, 0)


def _dispatch(hx, infot, tab, rows_tot, rmax):
    nblk, tb, w = hx.shape
    grid_spec = pltpu.PrefetchScalarGridSpec(
        num_scalar_prefetch=5,
        grid=(nblk,),
        in_specs=[pl.BlockSpec((1, tb, w), lambda i, *_: (i, 0, 0)),
                  pl.BlockSpec((1, 8, tb), lambda i, *_: (i, 0, 0))],
        out_specs=pl.BlockSpec(memory_space=pl.ANY),
        scratch_shapes=[pltpu.VMEM((2, rmax, w), BF16), pltpu.VMEM((MOE_SEG, w), BF16),
                        pltpu.SemaphoreType.DMA((2,)), pltpu.SemaphoreType.DMA((1,))])
    return pl.pallas_call(
        functools.partial(_dispatch_kernel, rmax=rmax),
        out_shape=jax.ShapeDtypeStruct((rows_tot, w), BF16),
        grid_spec=grid_spec,
        compiler_params=_params(("arbitrary",)),
        name="moe_dispatch",
    )(tab['loc_off'], tab['nchunk'], tab['off'], tab['zstart'], tab['znchunk'], hx, infot)


def _gffn_kernel(te_ref, nv_ref, x_ref, wg_ref, wu_ref, wd_ref, o_ref, acc_s):
    t = pl.program_id(0)
    f = pl.program_id(1)
    nf = pl.num_programs(1)
    d = o_ref.shape[1]

    @pl.when(t < nv_ref[0])
    def _():
        @pl.when(f == 0)
        def _():
            acc_s[...] = jnp.zeros_like(acc_s)

        hb = x_ref[:, :d]
        tf = wg_ref.shape[2]
        fsub = min(MOE_FSUB, tf)
        for c0 in range(0, tf, fsub):
            gt = _dot(hb, wg_ref[0, :, c0:c0 + fsub])
            up = _dot(hb, wu_ref[0, :, c0:c0 + fsub])
            act = (gt * jax.nn.sigmoid(gt) * up).astype(BF16)
            acc_s[...] += _dot(act, wd_ref[0, c0:c0 + fsub, :])

        @pl.when(f == nf - 1)
        def _():
            pieces = x_ref[:, d:].astype(F32)
            lane = lax.broadcasted_iota(jnp.int32, pieces.shape, 1)
            mine = (lane < 3 * N_EXPERTS) & (lane % N_EXPERTS == te_ref[t])
            gate = jnp.sum(jnp.where(mine, pieces, 0.0), axis=-1, keepdims=True)
            o_ref[...] = (acc_s[...] * gate).astype(BF16)

    @pl.when((t >= nv_ref[0]) & (f == nf - 1))
    def _():
        o_ref[...] = jnp.zeros_like(o_ref)


def _grouped_ffn(xs, tab, wg, wu, wd, tm, tf):
    rows, w = xs.shape
    d = wg.shape[1]
    n_tiles = rows // tm
    nf = wg.shape[2] // tf
    live = lambda t, nv: jnp.minimum(t, nv[0] - 1)
    grid_spec = pltpu.PrefetchScalarGridSpec(
        num_scalar_prefetch=2,
        grid=(n_tiles, nf),
        in_specs=[pl.BlockSpec((tm, w), lambda t, f, te, nv: (live(t, nv), 0)),
                  pl.BlockSpec((1, d, tf), lambda t, f, te, nv: (te[t], 0, f)),
                  pl.BlockSpec((1, d, tf), lambda t, f, te, nv: (te[t], 0, f)),
                  pl.BlockSpec((1, tf, d), lambda t, f, te, nv: (te[t], f, 0))],
        out_specs=pl.BlockSpec((tm, d), lambda t, f, te, nv: (t, 0)),
        scratch_shapes=[pltpu.VMEM((tm, d), F32)])
    return pl.pallas_call(
        _gffn_kernel,
        out_shape=jax.ShapeDtypeStruct((rows, d), BF16),
        grid_spec=grid_spec,
        compiler_params=_params(("arbitrary", "arbitrary")),
        name="moe_grouped_ffn",
    )(tab['tile_e'], tab['n_live'], xs, wg, wu, wd)


def _combine_kernel(lo_ref, nc_ref, off_ref, info_ref, ys_hbm, x_ref, gate_ref, gp_ref, o_ref,
                    yall, sem, *, rmax):
    i = pl.program_id(0)
    n = pl.num_programs(0)
    slot = i % 2
    tb = x_ref.shape[1]

    def chunk_copy(blk, sl, e, c):
        k = blk * N_EXPERTS + e
        src = pl.multiple_of(off_ref[k] + c * MOE_SEG, MOE_SEG)
        dst = pl.multiple_of(lo_ref[k] + c * MOE_SEG, MOE_SEG)
        return pltpu.make_async_copy(ys_hbm.at[pl.ds(src, MOE_SEG)],
                                     yall.at[sl, pl.ds(dst, MOE_SEG)], sem.at[sl])

    def fetch(blk, sl):
        yall[sl, 2 * tb:, :] = jnp.zeros((rmax - 2 * tb, yall.shape[2]), BF16)
        _segment_loops(blk, nc_ref, lambda e, c: chunk_copy(blk, sl, e, c).start())

    @pl.when(i == 0)
    def _():
        fetch(0, 0)

    @pl.when(i + 1 < n)
    def _():
        fetch(i + 1, 1 - slot)

    _segment_loops(i, nc_ref, lambda e, c: chunk_copy(i, slot, e, c).wait())
    rsub = min(MOE_CSUB, tb)
    for r0 in range(0, tb, rsub):
        info = info_ref[0, r0:r0 + rsub, :]
        loc1 = _local_rows(info[:, 0:1], info[:, 2:3], i, lo_ref)
        loc2 = _local_rows(info[:, 1:2], info[:, 3:4], i, lo_ref)
        lane = lax.broadcasted_iota(jnp.int32, (rsub, rmax), 1).astype(F32)
        sel = jnp.where((lane == loc1) | (lane == loc2), 1.0, 0.0).astype(BF16)
        o_ref[0, r0:r0 + rsub, :] = (x_ref[0, r0:r0 + rsub, :]
                                     + gate_ref[0] * _rms(_dot(sel, yall[slot]), gp_ref[...]))


def _combine(ys, info, tab, x3, gate, gp, bpb, rmax):
    nblk, tb, d = x3.shape
    grid_spec = pltpu.PrefetchScalarGridSpec(
        num_scalar_prefetch=3,
        grid=(nblk,),
        in_specs=[pl.BlockSpec((1, tb, LANES), lambda i, *_: (i, 0, 0)),
                  pl.BlockSpec(memory_space=pl.ANY),
                  pl.BlockSpec((1, tb, d), lambda i, *_: (i, 0, 0)),
                  pl.BlockSpec((1, 1, d), lambda i, *_: (i // bpb, 0, 0)),
                  pl.BlockSpec((1, d), lambda i, *_: (0, 0))],
        out_specs=pl.BlockSpec((1, tb, d), lambda i, *_: (i, 0, 0)),
        scratch_shapes=[pltpu.VMEM((2, rmax, d), BF16), pltpu.SemaphoreType.DMA((2,))])
    return pl.pallas_call(
        functools.partial(_combine_kernel, rmax=rmax),
        out_shape=jax.ShapeDtypeStruct((nblk, tb, d), F32),
        grid_spec=grid_spec,
        compiler_params=_params(("arbitrary",)),
        name="moe_combine",
    )(tab['loc_off'], tab['nchunk'], tab['off'], info, ys, x3, gate, gp)


def _moe(x, sc, sh, gpre, router, wg, wu, wd, gate, gpost, tm, tf):
    b, l, d = x.shape
    tb = min(MOE_TB, l)
    bpb = l // tb
    nblk = b * bpb
    x3 = x.reshape(nblk, tb, d)
    hx, info, infot, cnt = _router(x3, sc, sh, gpre, router, bpb)
    cnt = cnt[:, 0, :N_EXPERTS].astype(jnp.int32)
    seg_pad = N_EXPERTS * (MOE_SEG - 1)
    rmax = -(-(2 * tb + seg_pad) // 256) * 256
    n_tiles = -(-(2 * b * l + nblk * seg_pad + N_EXPERTS * (tm - 1)) // tm)
    tab = _routing_tables(cnt, tm, n_tiles)
    xs = _dispatch(hx, infot, tab, n_tiles * tm, rmax)
    ys = _grouped_ffn(xs, tab, wg, wu, wd, tm, tf)
    return _combine(ys, info, tab, x3, gate, gpost, bpb, rmax).reshape(b, l, d)


def _rope_tables(s):
    half = DIFF_DK // 2
    n_freq = half // 2
    inv = ROPE_THETA ** (-jnp.arange(n_freq, dtype=F32) / n_freq)
    t = jnp.arange(s, dtype=jnp.int32)
    rows = (t // GRID_W).astype(F32)[:, None]
    cols = (t % GRID_W).astype(F32)[:, None]
    lane = jnp.arange(LANES, dtype=jnp.int32)
    dd = lane % DIFF_DK
    pos = jnp.where((dd < half)[None, :], rows, cols)
    ang = pos * inv[dd % n_freq][None, :]
    first = ((dd % half) < n_freq)[None, :]
    cos, sin = jnp.cos(ang), jnp.sin(ang)
    return cos, jnp.where(first, -sin, 0.0), jnp.where(first, 0.0, sin)


def _dft_tables(n):
    m = 32 if n % 32 == 0 and n > 32 else 1
    j = jnp.arange(n, dtype=jnp.int32)[:, None]

    def tab(k, period):
        ang = ((j * k[None, :]) % period).astype(F32) * (2.0 * math.pi / period)
        return jnp.cos(ang)[:, :, None], jnp.sin(ang)[:, :, None]

    ca, sa = tab(jnp.arange(n // m, dtype=jnp.int32), n // m)
    cb, sb = tab(jnp.arange(m, dtype=jnp.int32), n)
    cb, sb = jnp.swapaxes(cb, 1, 2), jnp.swapaxes(sb, 1, 2)
    sc = n ** -0.5
    return (((ca * cb - sa * sb) * sc).reshape(n, n), ((sa * cb + ca * sb) * sc).reshape(n, n))


def _channel_dft():
    c, s = _dft_tables(FNET_GROUP_CH)
    eye = jnp.eye(FNET_GROUPS, dtype=F32)
    return jnp.concatenate([jnp.kron(eye, c), jnp.kron(eye, s)], axis=1).astype(BF16)


def kernel(x, c, ctx, c_ctx, ada_w, ada_b, norm_mix_pre, norm_mix_post, norm_ffn_pre, norm_ffn_post,
           w_in, w_out, diff_lq1, diff_lk1, diff_lq2, diff_lk2, diff_subln, fnet_w,
           s5_a_re, s5_a_im, s5_log_dt, s5_b_re, s5_b_im, s5_c_re, s5_c_im, s5_d, s5_w_glu,
           ffn_w_gate, ffn_w_up, ffn_w_down, moe_router, moe_w_gate, moe_w_up, moe_w_down):
    b, s, d = x.shape
    lc = ctx.shape[1]
    depth = ada_w.shape[0]

    cc = jnp.zeros((MOD_ROWS, d), F32).at[:b].set(c).at[b].set(c_ctx)
    mod = _modulation(cc, ada_w, ada_b)

    rope_tabs = _rope_tables(s)
    cs64 = _channel_dft()
    dft = {n: tuple(t.astype(BF16) for t in _dft_tables(n)) for n in (s, lc)}

    xc = ctx
    for l in range(depth):
        need_ctx = l < depth - 1
        lam_init = 0.8 - 0.6 * math.exp(-0.3 * l)
        m_lat = [mod[l, :b, i * d:(i + 1) * d].reshape(b, 1, d) for i in range(6)]
        m_ctx = [jnp.broadcast_to(mod[l, b, i * d:(i + 1) * d].reshape(1, 1, d), (b, 1, d))
                 for i in range(6)]
        row = lambda v: v.reshape(1, -1).astype(F32)
        w_in_bf = w_in[l].astype(BF16)
        w_out_bf = w_out[l].astype(BF16)
        fw_bf = fnet_w[l].astype(BF16)
        wglu_bf = s5_w_glu[l].astype(BF16)
        lams = (row(diff_lq1[l]), row(diff_lk1[l]), row(diff_lq2[l]), row(diff_lk2[l]))
        subln = row(diff_subln[l])

        q, k, v, g1, g2, u = _inproj(x, m_lat[1], m_lat[0], row(norm_mix_pre[l]), w_in_bf, cs64,
                                     rope_tabs, min(INPROJ_TL, s))
        qc, kc, vc, g1c, g2c, uc = _inproj(xc, m_ctx[1], m_ctx[0], row(norm_mix_pre[l]), w_in_bf,
                                           cs64, None, lc)
        a_lat = _attention(q, [(kc, vc), (k, v)], lams, subln, lam_init, min(ATTN_TQ, s))
        f_lat = _fnet_dft(*dft[s], g1, g2, fw_bf, b, min(512, s), 2)

        mats = [_s5_matrices(s5_a_re[l, dr], s5_a_im[l, dr], s5_log_dt[l, dr], s5_b_re[l, dr],
                             s5_b_im[l, dr], s5_c_re[l, dr], s5_c_im[l, dr]) for dr in (0, 1)]
        mats = (mats[0][0], mats[1][0], mats[0][1], mats[1][1], mats[0][2], mats[1][2])
        zero_state = jnp.zeros((b, 2 * S5_LANES), F32)
        ycf, ycr, hcf, hcr = _s5_scan(uc, mats, zero_state, zero_state, b, S5_TC)
        yf, yr, _, _ = _s5_scan(u, mats, hcf, hcr, b, S5_TC)

        x = _outproj(a_lat, f_lat, yf, yr, u, row(s5_d[l]), wglu_bf, w_out_bf,
                     x, m_lat[2], row(norm_mix_post[l]), min(512, s))
        if need_ctx:
            a_ctx = _attention(qc, [(kc, vc)], lams, subln, lam_init, lc)
            f_ctx = _fnet_dft(*dft[lc], g1c, g2c, fw_bf, b, lc, 2)
            xc = _outproj(a_ctx, f_ctx, ycf, ycr, uc, row(s5_d[l]), wglu_bf,
                          w_out_bf, xc, m_ctx[2], row(norm_mix_post[l]), lc)

        i = l // 2
        gpre, gpost = row(norm_ffn_pre[l]), row(norm_ffn_post[l])
        xc1 = xc.reshape(1, b * lc, d)
        if l % 2 == 0:
            wg, wu, wd = (ffn_w_gate[i].astype(BF16), ffn_w_up[i].astype(BF16),
                          ffn_w_down[i].astype(BF16))
            tf = FFN_DENSE_TF
            x = _ffn(x, m_lat[4], m_lat[3], gpre, wg, wu, wd, m_lat[5], gpost, min(1024, s), tf)
            if need_ctx:
                xc1 = _ffn(xc1, m_ctx[4][:1], m_ctx[3][:1], gpre, wg, wu, wd, m_ctx[5][:1], gpost,
                           min(1024, b * lc), tf)
        else:
            wg, wu, wd = (moe_w_gate[i].astype(BF16), moe_w_up[i].astype(BF16),
                          moe_w_down[i].astype(BF16))
            r = jnp.zeros((d, LANES), F32).at[:, :N_EXPERTS].set(moe_router[i])
            r_hi = r.astype(BF16)
            router = jnp.stack([r_hi, (r - r_hi.astype(F32)).astype(BF16)])
            x = _moe(x, m_lat[4], m_lat[3], gpre, router, wg, wu, wd, m_lat[5], gpost,
                     MOE_TM, MOE_TF)
            if need_ctx:
                xc1 = _moe(xc1, m_ctx[4][:1], m_ctx[3][:1], gpre, router, wg, wu, wd, m_ctx[5][:1],
                           gpost, MOE_TM, MOE_TF)
        xc = xc1.reshape(b, lc, d)
    return x
```

```python
import functools
import math

import jax
import jax.numpy as jnp
from jax import lax
from jax.experimental import pallas as pl
from jax.experimental.pallas import tpu as pltpu

F32 = jnp.float32
BF16 = jnp.bfloat16

D_MODEL = 1024
DEPTH = 2
GRID_W = 64
EPS = 1e-6
DIFF_HEADS = 4
DIFF_DK = 64
DIFF_DV = 2 * DIFF_DK
DIFF_WIDTH = DIFF_HEADS * DIFF_DV
DIFF_QK_WIDTH = DIFF_HEADS * 2 * DIFF_DK
ROPE_THETA = 10000.0
FNET_GROUPS = 4
FNET_GROUP_CH = 64
FNET_WIDTH = FNET_GROUPS * FNET_GROUP_CH
S5_CH = 16
S5_GROUPS = 16
S5_STATE = 64
S5_WIDTH = S5_GROUPS * S5_CH
S5_LANES = S5_GROUPS * S5_STATE
IN_WIDTH = 2 * DIFF_QK_WIDTH + DIFF_WIDTH + FNET_WIDTH + S5_WIDTH
N_EXPERTS = 8
LANES = 128
LOG2E = math.log2(math.e)
MOD_ROWS = 24
FFN_DENSE_TF = 1408
INPROJ_TL = 1024
INPROJ_RSUB = 256
ATTN_TQ = 2048
ATTN_RSUB = 128
S5_TC = 32
MOE_TB = 1024
MOE_SEG = 32
MOE_RSUB = 768
MOE_CSUB = 256
MOE_TM = 1024
MOE_TF = 1792
MOE_FSUB = 896

VMEM_LIMIT = 56 * 1024 * 1024


def _params(sem):
    return pltpu.CompilerParams(dimension_semantics=sem, vmem_limit_bytes=VMEM_LIMIT)


def _rms(x, g):
    return x * lax.rsqrt(jnp.mean(x * x, axis=-1, keepdims=True) + EPS) * g


def _dot(a, b):
    return jnp.dot(a, b, preferred_element_type=F32)


def _dot_row_halves(a, b):
    half = a.shape[0] // 2
    return jnp.concatenate([_dot(a[:half, :], b), _dot(a[half:, :], b)], axis=0)


def _mod_kernel(c_ref, w_ref, b_ref, o_ref):
    c = c_ref[...]
    sc = c * jax.nn.sigmoid(c)
    o_ref[0] = jnp.dot(sc, w_ref[0], preferred_element_type=F32,
                       precision=lax.Precision.HIGHEST) + b_ref[0]


def _modulation(cc, ada_w, ada_b):
    depth, d, n = ada_w.shape
    tn = 1536
    return pl.pallas_call(
        _mod_kernel,
        out_shape=jax.ShapeDtypeStruct((depth, MOD_ROWS, n), F32),
        grid=(depth, n // tn),
        in_specs=[pl.BlockSpec((MOD_ROWS, d), lambda l, j: (0, 0)),
                  pl.BlockSpec((1, d, tn), lambda l, j: (l, 0, j)),
                  pl.BlockSpec((1, 1, tn), lambda l, j: (l, 0, j))],
        out_specs=pl.BlockSpec((1, MOD_ROWS, tn), lambda l, j: (l, 0, j)),
        compiler_params=_params(("parallel", "parallel")),
        name="adaln_mod",
    )(cc, ada_w, ada_b.reshape(depth, 1, n))


def _inproj_kernel(*refs, rope):
    if rope:
        (x_ref, sc_ref, sh_ref, g_ref, w_ref, cs_ref, cos_ref, sina_ref, sinb_ref,
         q_ref, k_ref, v_ref, g1_ref, g2_ref, u_ref) = refs
    else:
        (x_ref, sc_ref, sh_ref, g_ref, w_ref, cs_ref,
         q_ref, k_ref, v_ref, g1_ref, g2_ref, u_ref) = refs
    tl = x_ref.shape[1]
    rsub = min(INPROJ_RSUB, tl)
    for r0 in range(0, tl, rsub):
        rows = slice(r0, r0 + rsub)
        h = _rms(x_ref[0, rows, :], g_ref[...]) * (1.0 + sc_ref[0]) + sh_ref[0]
        hb = h.astype(BF16)

        def proj(lo, hi):
            return _dot(hb, w_ref[:, lo:hi])

        def rotate(t):
            outs = []
            for j in range(t.shape[1] // LANES):
                tb = t[:, j * LANES:(j + 1) * LANES]
                outs.append(tb * cos_ref[rows, :]
                            + pltpu.roll(tb, LANES - 16, axis=1) * sina_ref[rows, :]
                            + pltpu.roll(tb, 16, axis=1) * sinb_ref[rows, :])
            return jnp.concatenate(outs, axis=1)

        q = proj(0, DIFF_QK_WIDTH)
        k = proj(DIFF_QK_WIDTH, 2 * DIFF_QK_WIDTH)
        if rope:
            q = rotate(q)
            k = rotate(k)
        q_ref[0, rows, :] = (q * (DIFF_DK ** -0.5 * LOG2E)).astype(BF16)
        k_ref[0, rows, :] = k.astype(BF16)
        o = 2 * DIFF_QK_WIDTH
        v_ref[0, rows, :] = proj(o, o + DIFF_WIDTH).astype(BF16)
        o += DIFF_WIDTH
        f = proj(o, o + FNET_WIDTH).astype(BF16)
        g12 = _dot(f, cs_ref[...])
        g1_ref[rows, :] = g12[:, :FNET_WIDTH].astype(BF16)
        g2_ref[rows, :] = g12[:, FNET_WIDTH:].astype(BF16)
        u_ref[rows, :] = proj(o + FNET_WIDTH, IN_WIDTH)


def _inproj(x, sc, sh, g, w_bf, cs64, rope_tabs, tl):
    b, l, d = x.shape
    rope = rope_tabs is not None
    in_specs = [pl.BlockSpec((1, tl, d), lambda i, bb: (bb, i, 0)),
                pl.BlockSpec((1, 1, d), lambda i, bb: (bb, 0, 0)),
                pl.BlockSpec((1, 1, d), lambda i, bb: (bb, 0, 0)),
                pl.BlockSpec((1, d), lambda i, bb: (0, 0)),
                pl.BlockSpec((d, IN_WIDTH), lambda i, bb: (0, 0)),
                pl.BlockSpec((FNET_WIDTH, 2 * FNET_WIDTH), lambda i, bb: (0, 0))]
    args = [x, sc, sh, g, w_bf, cs64]
    if rope:
        in_specs += [pl.BlockSpec((tl, LANES), lambda i, bb: (i, 0))] * 3
        args += list(rope_tabs)
    tok = lambda w: pl.BlockSpec((1, tl, w), lambda i, bb: (bb, i, 0))
    tb = pl.BlockSpec((tl, FNET_WIDTH), lambda i, bb: (i, bb))
    return pl.pallas_call(
        functools.partial(_inproj_kernel, rope=rope),
        out_shape=(jax.ShapeDtypeStruct((b, l, DIFF_QK_WIDTH), BF16),
                   jax.ShapeDtypeStruct((b, l, DIFF_QK_WIDTH), BF16),
                   jax.ShapeDtypeStruct((b, l, DIFF_WIDTH), BF16),
                   jax.ShapeDtypeStruct((l, b * FNET_WIDTH), BF16),
                   jax.ShapeDtypeStruct((l, b * FNET_WIDTH), BF16),
                   jax.ShapeDtypeStruct((l, b * S5_WIDTH), F32)),
        grid=(l // tl, b),
        in_specs=in_specs,
        out_specs=(tok(DIFF_QK_WIDTH), tok(DIFF_QK_WIDTH), tok(DIFF_WIDTH), tb, tb, tb),
        compiler_params=_params(("parallel", "parallel")),
        name="inproj_rope" if rope else "inproj",
    )(*args)


def _attn_kernel(*refs, n_src, lam_init, rsub):
    q_ref = refs[0]
    kv = refs[1:1 + 2 * n_src]
    lq1, lk1, lq2, lk2, sub_ref, o_ref, k1_s, k2_s, v_s = refs[1 + 2 * n_src:]

    @pl.when(pl.program_id(2) == 0)
    def _():
        off = 0
        for s in range(n_src):
            kk = kv[2 * s][0]
            n = kk.shape[0]
            k1_s[off:off + n, :] = kk[:, :DIFF_DK]
            k2_s[off:off + n, :] = kk[:, DIFF_DK:]
            v_s[off:off + n, :DIFF_DV] = kv[2 * s + 1][0]
            off += n
        v_s[:, DIFF_DV:] = jnp.ones((v_s.shape[0], DIFF_DV), BF16)

    lam = (jnp.exp(jnp.sum(lq1[...] * lk1[...], axis=-1, keepdims=True))
           - jnp.exp(jnp.sum(lq2[...] * lk2[...], axis=-1, keepdims=True)) + lam_init)
    def attend(qj, k_s):
        s = lax.dot_general(qj, k_s[...], (((1,), (1,)), ((), ())), preferred_element_type=F32)
        p = jnp.exp2(s - jnp.max(s, axis=-1, keepdims=True)).astype(BF16)
        ol = _dot(p, v_s[...])
        return ol[:, :DIFF_DV] / ol[:, DIFF_DV:DIFF_DV + 1]

    for r0 in range(0, q_ref.shape[1], rsub):
        q = q_ref[0, r0:r0 + rsub, :]
        o = attend(q[:, :DIFF_DK], k1_s) - lam * attend(q[:, DIFF_DK:], k2_s)
        o_ref[0, r0:r0 + rsub, :] = (_rms(o, sub_ref[...]) * (1.0 - lam_init)).astype(BF16)


def _attention(q, kv_srcs, lams, subln, lam_init, tq):
    b, lq, _ = q.shape
    n_src = len(kv_srcs)
    lk = sum(k.shape[1] for k, _ in kv_srcs)
    in_specs = [pl.BlockSpec((1, tq, DIFF_DV), lambda bb, h, i: (bb, i, h))]
    args = [q]
    for k, v in kv_srcs:
        spec = pl.BlockSpec((1, k.shape[1], DIFF_DV), lambda bb, h, i: (bb, 0, h))
        in_specs += [spec, spec]
        args += [k, v]
    in_specs += [pl.BlockSpec((1, DIFF_DK), lambda bb, h, i: (0, 0))] * 4
    in_specs += [pl.BlockSpec((1, DIFF_DV), lambda bb, h, i: (0, 0))]
    args += list(lams) + [subln]
    return pl.pallas_call(
        functools.partial(_attn_kernel, n_src=n_src, lam_init=lam_init, rsub=min(ATTN_RSUB, tq)),
        out_shape=jax.ShapeDtypeStruct((b, lq, DIFF_WIDTH), BF16),
        grid=(b, DIFF_HEADS, lq // tq),
        in_specs=in_specs,
        out_specs=pl.BlockSpec((1, tq, DIFF_DV), lambda bb, h, i: (bb, i, h)),
        scratch_shapes=[pltpu.VMEM((lk, DIFF_DK), BF16), pltpu.VMEM((lk, DIFF_DK), BF16),
                        pltpu.VMEM((lk, 2 * DIFF_DV), BF16)],
        compiler_params=_params(("parallel", "parallel", "arbitrary")),
        name="diff_attn_%d" % n_src,
    )(*args)


def _dft_kernel(c_ref, s_ref, g1_ref, g2_ref, w_ref, o_ref):
    z = _dot(c_ref[...], g1_ref[...]) - _dot(s_ref[...], g2_ref[...])
    for j in range(o_ref.shape[0]):
        zj = z[:, j * FNET_WIDTH:(j + 1) * FNET_WIDTH].astype(BF16)
        o_ref[j] = _dot(zj, w_ref[...]).astype(BF16)


def _fnet_dft(cosm, sinm, g1, g2, w_bf, b, tm, nb):
    l = cosm.shape[0]
    tn = nb * FNET_WIDTH
    return pl.pallas_call(
        _dft_kernel,
        out_shape=jax.ShapeDtypeStruct((b, l, FNET_WIDTH), BF16),
        grid=(l // tm, b // nb),
        in_specs=[pl.BlockSpec((tm, l), lambda i, j: (i, 0)),
                  pl.BlockSpec((tm, l), lambda i, j: (i, 0)),
                  pl.BlockSpec((l, tn), lambda i, j: (0, j)),
                  pl.BlockSpec((l, tn), lambda i, j: (0, j)),
                  pl.BlockSpec((FNET_WIDTH, FNET_WIDTH), lambda i, j: (0, 0))],
        out_specs=pl.BlockSpec((nb, tm, FNET_WIDTH), lambda i, j: (j, i, 0)),
        compiler_params=_params(("parallel", "parallel")),
        name="fnet_dft",
    )(cosm, sinm, g1, g2, w_bf)


def _s5_kernel(uf_ref, ur_ref, perm_ref, permt_ref, bf_ref, br_ref, lf_ref, lr_ref, cf_ref, cr_ref,
               h0f_ref, h0r_ref, yf_ref, yr_ref, hef_ref, her_ref, hs_f, hs_r, *, tc, nb):
    j = pl.program_id(0)

    @pl.when(j == 0)
    def _():
        hs_f[...] = h0f_ref[...]
        hs_r[...] = h0r_ref[...]

    def drive(u_ref, b_ref):
        u_bt = jnp.concatenate(
            [u_ref[:, bb * S5_WIDTH:(bb + 1) * S5_WIDTH] for bb in range(nb)], axis=0).astype(BF16)
        u_tb = _dot_row_halves(perm_ref, u_bt).astype(BF16)
        return _dot(u_tb, b_ref[...])

    def scan(drv, l_ref, hs, reverse):
        hr, hi = hs[:, :S5_LANES], hs[:, S5_LANES:]
        re_rows, im_rows = [None] * tc, [None] * tc
        for t in (range(tc - 1, -1, -1) if reverse else range(tc)):
            lre = l_ref[:, :S5_LANES]
            lim = l_ref[:, S5_LANES:]
            d = drv[t * nb:(t + 1) * nb, :]
            hr, hi = (lre * hr - lim * hi + d[:, :S5_LANES], lre * hi + lim * hr + d[:, S5_LANES:])
            re_rows[t] = hr.astype(BF16)
            im_rows[t] = hi.astype(BF16)
        hs[:, :S5_LANES] = hr
        hs[:, S5_LANES:] = hi
        return jnp.concatenate(re_rows, axis=0), jnp.concatenate(im_rows, axis=0)

    def readout(h, c_ref, y_ref):
        half = h[0].shape[0] // 2
        y = jnp.concatenate(
            [_dot(h[0][r0:r0 + half], c_ref[:S5_LANES, :]) + _dot(h[1][r0:r0 + half], c_ref[S5_LANES:, :])
             for r0 in (0, half)], axis=0)
        y_hi = y.astype(BF16)
        y_lo = (y - y_hi.astype(F32)).astype(BF16)
        y_bt = _dot_row_halves(permt_ref, y_hi) + _dot_row_halves(permt_ref, y_lo)
        for bb in range(nb):
            y_ref[:, bb * S5_WIDTH:(bb + 1) * S5_WIDTH] = y_bt[bb * tc:(bb + 1) * tc, :]

    drv_f = drive(uf_ref, bf_ref)
    drv_r = drive(ur_ref, br_ref)
    readout(scan(drv_f, lf_ref, hs_f, False), cf_ref, yf_ref)
    readout(scan(drv_r, lr_ref, hs_r, True), cr_ref, yr_ref)

    @pl.when(j == pl.num_programs(0) - 1)
    def _():
        hef_ref[...] = hs_f[...]
        her_ref[...] = hs_r[...]


def _s5_scan(u, mats, h0f, h0r, nb, tc):
    l = u.shape[0]
    n = l // tc
    r = tc * nb
    bmf, bmr, lf, lr, cmf, cmr = mats
    lf, lr = (jnp.broadcast_to(v, (nb, v.shape[1])) for v in (lf, lr))
    rows = jnp.arange(r, dtype=jnp.int32)
    perm = (rows[None, :] == ((rows % nb) * tc + rows // nb)[:, None]).astype(BF16)
    permt = perm.T
    full = lambda a: pl.BlockSpec(a.shape, lambda j: (0,) * a.ndim)
    fwd = pl.BlockSpec((tc, nb * S5_WIDTH), lambda j: (j, 0))
    rev = pl.BlockSpec((tc, nb * S5_WIDTH), lambda j: (n - 1 - j, 0))
    st = jax.ShapeDtypeStruct((nb, 2 * S5_LANES), F32)
    return pl.pallas_call(
        functools.partial(_s5_kernel, tc=tc, nb=nb),
        out_shape=(jax.ShapeDtypeStruct(u.shape, F32), jax.ShapeDtypeStruct(u.shape, F32), st, st),
        grid=(n,),
        in_specs=[fwd, rev, full(perm), full(permt), full(bmf), full(bmr), full(lf), full(lr),
                  full(cmf), full(cmr), full(h0f), full(h0r)],
        out_specs=(fwd, rev, full(h0f), full(h0r)),
        scratch_shapes=[pltpu.VMEM((nb, 2 * S5_LANES), F32), pltpu.VMEM((nb, 2 * S5_LANES), F32)],
        compiler_params=_params(("arbitrary",)),
        name="s5_scan",
    )(u, u, perm, permt, bmf, bmr, lf, lr, cmf, cmr, h0f, h0r)


def _s5_matrices(a_re, a_im, log_dt, b_re, b_im, c_re, c_im):
    dt = jnp.exp(log_dt)[:, None]
    mag = jnp.exp(a_re * dt)
    lr, li = mag * jnp.cos(a_im * dt), mag * jnp.sin(a_im * dt)
    nr, ni = lr - 1.0, li
    den = a_re * a_re + a_im * a_im
    cr = (nr * a_re + ni * a_im) / den
    ci = (ni * a_re - nr * a_im) / den
    bbr = cr[..., None] * b_re - ci[..., None] * b_im
    bbi = cr[..., None] * b_im + ci[..., None] * b_re
    eye = jnp.eye(S5_GROUPS, dtype=F32)

    def drive_mat(bb):
        return jnp.einsum('gpc,gh->gchp', bb, eye).reshape(S5_WIDTH, S5_LANES)

    def read_mat(cc):
        return jnp.einsum('gcp,gh->gphc', cc, eye).reshape(S5_LANES, S5_WIDTH)

    bm = jnp.concatenate([drive_mat(bbr), drive_mat(bbi)], axis=1).astype(BF16)
    cm = jnp.concatenate([read_mat(c_re), -read_mat(c_im)], axis=0).astype(BF16)
    lam = jnp.concatenate([lr.reshape(1, S5_LANES), li.reshape(1, S5_LANES)], axis=1)
    return bm, lam, cm


def _outproj_kernel(a_ref, f_ref, yf_ref, yr_ref, u_ref, d_ref, wglu_ref, wa_ref, wf_ref, ws_ref,
                    x_ref, gate_ref, g_ref, o_ref):
    yy = yf_ref[...] + yr_ref[...] + d_ref[...] * u_ref[...]
    yy = jax.nn.gelu(yy)
    s = yy * jax.nn.sigmoid(_dot(yy.astype(BF16), wglu_ref[...]))
    mix = (_dot(a_ref[0], wa_ref[...]) + _dot(f_ref[0], wf_ref[...])
           + _dot(s.astype(BF16), ws_ref[...]))
    o_ref[0] = x_ref[0] + gate_ref[0] * _rms(mix, g_ref[...])


def _outproj(a, fo, yf, yr, u, d, wglu, w_out, x, gate, g, tl):
    b, l, dm = x.shape
    wa, wf, ws = (w_out[:DIFF_WIDTH], w_out[DIFF_WIDTH:DIFF_WIDTH + FNET_WIDTH],
                  w_out[DIFF_WIDTH + FNET_WIDTH:])
    tok = lambda w: pl.BlockSpec((1, tl, w), lambda i, bb: (bb, i, 0))
    tb = pl.BlockSpec((tl, S5_WIDTH), lambda i, bb: (i, bb))
    full = lambda arr: pl.BlockSpec(arr.shape, lambda i, bb: (0,) * arr.ndim)
    return pl.pallas_call(
        _outproj_kernel,
        out_shape=jax.ShapeDtypeStruct((b, l, dm), F32),
        grid=(l // tl, b),
        in_specs=[tok(DIFF_WIDTH), tok(FNET_WIDTH), tb, tb, tb, full(d), full(wglu),
                  full(wa), full(wf), full(ws), tok(dm),
                  pl.BlockSpec((1, 1, dm), lambda i, bb: (bb, 0, 0)), full(g)],
        out_specs=tok(dm),
        compiler_params=_params(("parallel", "parallel")),
        name="outproj",
    )(a, fo, yf, yr, u, d, wglu, wa, wf, ws, x, gate, g)


def _ffn_kernel(x_ref, sc_ref, sh_ref, g_ref, wg_ref, wu_ref, wd_ref, gate_ref, gp_ref,
                o_ref, h_s, acc_s):
    f = pl.program_id(2)

    @pl.when(f == 0)
    def _():
        h = _rms(x_ref[0], g_ref[...]) * (1.0 + sc_ref[0]) + sh_ref[0]
        h_s[...] = h.astype(BF16)
        acc_s[...] = jnp.zeros_like(acc_s)

    hb = h_s[...]
    gt = _dot(hb, wg_ref[...])
    up = _dot(hb, wu_ref[...])
    act = (gt * jax.nn.sigmoid(gt) * up).astype(BF16)
    acc_s[...] += _dot(act, wd_ref[...])

    @pl.when(f == pl.num_programs(2) - 1)
    def _():
        o_ref[0] = x_ref[0] + gate_ref[0] * _rms(acc_s[...], gp_ref[...])


def _ffn(x, sc, sh, g, wg, wu, wd, gate, gp, tm, tf):
    b, l, d = x.shape
    fdim = wg.shape[1]
    tok = pl.BlockSpec((1, tm, d), lambda bb, i, f: (bb, i, 0))
    modv = pl.BlockSpec((1, 1, d), lambda bb, i, f: (bb, 0, 0))
    vec = pl.BlockSpec((1, d), lambda bb, i, f: (0, 0))
    return pl.pallas_call(
        _ffn_kernel,
        out_shape=jax.ShapeDtypeStruct((b, l, d), F32),
        grid=(b, l // tm, fdim // tf),
        in_specs=[tok, modv, modv, vec,
                  pl.BlockSpec((d, tf), lambda bb, i, f: (0, f)),
                  pl.BlockSpec((d, tf), lambda bb, i, f: (0, f)),
                  pl.BlockSpec((tf, d), lambda bb, i, f: (f, 0)),
                  modv, vec],
        out_specs=tok,
        scratch_shapes=[pltpu.VMEM((tm, d), BF16), pltpu.VMEM((tm, d), F32)],
        compiler_params=_params(("parallel", "parallel", "arbitrary")),
        name="dense_ffn",
    )(x, sc, sh, g, wg, wu, wd, gate, gp)


def _router_kernel(x_ref, sc_ref, sh_ref, g_ref, r_ref, tri_ref, hx_ref, info_ref, infot_ref, cnt_ref):
    d = x_ref.shape[2]
    h = _rms(x_ref[0], g_ref[...]) * (1.0 + sc_ref[0]) + sh_ref[0]
    hi = h.astype(BF16)
    lo = (h - hi.astype(F32)).astype(BF16)
    logits = (_dot_row_halves(hi, r_ref[0]) + _dot_row_halves(hi, r_ref[1])
              + _dot_row_halves(lo, r_ref[0]))
    lane = lax.broadcasted_iota(jnp.int32, logits.shape, 1).astype(F32)
    neg = jnp.float32(-jnp.inf)
    logits = jnp.where(lane < N_EXPERTS, logits, neg)
    m1 = jnp.max(logits, axis=-1, keepdims=True)
    i1 = jnp.min(jnp.where(logits == m1, lane, float(LANES)), axis=-1, keepdims=True)
    rest = jnp.where(lane == i1, neg, logits)
    m2 = jnp.max(rest, axis=-1, keepdims=True)
    i2 = jnp.min(jnp.where(rest == m2, lane, float(LANES)), axis=-1, keepdims=True)
    e2 = jnp.exp(m2 - m1)
    w1 = 1.0 / (1.0 + e2)
    oh1 = lane == i1
    oh2 = lane == i2
    member = jnp.where(oh1 | oh2, 1.0, 0.0)
    ranks = _dot_row_halves(tri_ref, member.astype(BF16))
    pos1 = jnp.sum(jnp.where(oh1, ranks, 0.0), axis=-1, keepdims=True)
    pos2 = jnp.sum(jnp.where(oh2, ranks, 0.0), axis=-1, keepdims=True)
    cnt_ref[0] = jnp.broadcast_to(jnp.sum(member, axis=0, keepdims=True), cnt_ref.shape[1:])
    gates = jnp.where(oh1, w1, 0.0) + jnp.where(oh2, e2 * w1, 0.0)
    g_hi = gates.astype(BF16).astype(F32)
    g_mid = (gates - g_hi).astype(BF16).astype(F32)
    g_lo = gates - g_hi - g_mid
    pieces = g_hi + pltpu.roll(g_mid, N_EXPERTS, axis=1) + pltpu.roll(g_lo, 2 * N_EXPERTS, axis=1)
    hx_ref[0, :, :d] = hi
    hx_ref[0, :, d:] = pieces.astype(BF16)
    info = jnp.where(lane == 0, i1, jnp.where(lane == 1, i2, jnp.where(
        lane == 2, pos1, jnp.where(lane == 3, pos2, 0.0))))
    info_ref[0] = info
    infot_ref[0] = info.T[:8, :]


def _router(x3, sc, sh, g, router, bpb):
    nblk, tb, d = x3.shape
    tok = lambda w: pl.BlockSpec((1, tb, w), lambda i: (i, 0, 0))
    modv = pl.BlockSpec((1, 1, d), lambda i: (i // bpb, 0, 0))
    r = jnp.arange(tb, dtype=jnp.int32)
    tri = (r[None, :] < r[:, None]).astype(BF16)
    return pl.pallas_call(
        _router_kernel,
        out_shape=(jax.ShapeDtypeStruct((nblk, tb, d + LANES), BF16),
                   jax.ShapeDtypeStruct((nblk, tb, LANES), F32),
                   jax.ShapeDtypeStruct((nblk, 8, tb), F32),
                   jax.ShapeDtypeStruct((nblk, 8, LANES), F32)),
        grid=(nblk,),
        in_specs=[tok(d), modv, modv, pl.BlockSpec((1, d), lambda i: (0, 0)),
                  pl.BlockSpec(router.shape, lambda i: (0, 0, 0)),
                  pl.BlockSpec((tb, tb), lambda i: (0, 0))],
        out_specs=(tok(d + LANES), tok(LANES), pl.BlockSpec((1, 8, tb), lambda i: (i, 0, 0)),
                   pl.BlockSpec((1, 8, LANES), lambda i: (i, 0, 0))),
        compiler_params=_params(("parallel",)),
        name="moe_router",
    )(x3, sc, sh, g, router, tri)


def _routing_tables(cnt, tm, n_tiles):
    i32 = jnp.int32
    pcnt = ((cnt + MOE_SEG - 1) // MOE_SEG) * MOE_SEG
    loc_off = jnp.cumsum(pcnt, axis=1) - pcnt
    used = jnp.sum(pcnt, axis=0)
    region = ((used + tm - 1) // tm) * tm
    ends = jnp.cumsum(region)
    base = ends - region
    off = base[None, :] + jnp.cumsum(pcnt, axis=0) - pcnt
    tile_row0 = jnp.arange(n_tiles, dtype=i32) * tm
    tile_e = jnp.minimum(jnp.sum((ends[None, :] <= tile_row0[:, None]).astype(i32), axis=1),
                         N_EXPERTS - 1)
    flat = lambda a: a.reshape(-1).astype(i32)
    zstart = jnp.concatenate([base + used, ends[-1:]])
    zrows = jnp.concatenate([region - used, n_tiles * tm - ends[-1:]])
    return dict(loc_off=flat(loc_off), nchunk=flat(pcnt // MOE_SEG), off=flat(off),
                zstart=flat(zstart), znchunk=flat(zrows // MOE_SEG),
                tile_e=tile_e.astype(i32), n_live=(ends[-1:] // tm).astype(i32))


def _segment_loops(blk, nchunk_ref, fn):
    for e in range(N_EXPERTS):
        def body(c, carry, e=e):
            fn(e, c)
            return carry
        lax.fori_loop(0, nchunk_ref[blk * N_EXPERTS + e], body, 0)


def _local_rows(idx, pos, blk, lo_ref):
    loc = pos
    for e in range(N_EXPERTS):
        loc = loc + jnp.where(idx == float(e), lo_ref[blk * N_EXPERTS + e].astype(F32), 0.0)
    return loc


def _dispatch_kernel(lo_ref, nc_ref, off_ref, zs_ref, zn_ref, hx_ref, it_ref, xs_hbm,
                     xall, zbuf, sem, zsem, *, rmax):
    i = pl.program_id(0)
    n = pl.num_programs(0)
    slot = i % 2

    def chunk_copy(blk, sl, e, c):
        k = blk * N_EXPERTS + e
        src = pl.multiple_of(lo_ref[k] + c * MOE_SEG, MOE_SEG)
        dst = pl.multiple_of(off_ref[k] + c * MOE_SEG, MOE_SEG)
        return pltpu.make_async_copy(xall.at[sl, pl.ds(src, MOE_SEG)],
                                     xs_hbm.at[pl.ds(dst, MOE_SEG)], sem.at[sl])

    def zero_copy(e, c):
        dst = pl.multiple_of(zs_ref[e] + c * MOE_SEG, MOE_SEG)
        return pltpu.make_async_copy(zbuf, xs_hbm.at[pl.ds(dst, MOE_SEG)], zsem.at[0])

    @pl.when(i >= 2)
    def _():
        _segment_loops(i - 2, nc_ref, lambda e, c: chunk_copy(i - 2, slot, e, c).wait())

    it = it_ref[0]
    loc1 = _local_rows(it[0:1, :], it[2:3, :], i, lo_ref)
    loc2 = _local_rows(it[1:2, :], it[3:4, :], i, lo_ref)
    hx = hx_ref[0]
    rsub = MOE_RSUB if rmax % MOE_RSUB == 0 else rmax
    for r0 in range(0, rmax, rsub):
        row = (lax.broadcasted_iota(jnp.int32, (rsub, it.shape[1]), 0) + r0).astype(F32)
        sel = jnp.where((row == loc1) | (row == loc2), 1.0, 0.0).astype(BF16)
        xall[slot, r0:r0 + rsub, :] = _dot(sel, hx).astype(BF16)
    _segment_loops(i, nc_ref, lambda e, c: chunk_copy(i, slot, e, c).start())

    @pl.when(i == n - 1)
    def _():
        zbuf[...] = jnp.zeros_like(zbuf)
        for z in range(N_EXPERTS + 1):
            def zbody(c, carry, z=z):
                zero_copy(z, c).start()
                return carry
            lax.fori_loop(0, zn_ref[z], zbody, 0)

        @pl.when(n >= 2)
        def _():
            _segment_loops(i - 1, nc_ref, lambda e, c: chunk_copy(i - 1, 1 - slot, e, c).wait())
        _segment_loops(i, nc_ref, lambda e, c: chunk_copy(i, slot, e, c).wait())
        for z in range(N_EXPERTS + 1):
            def wbody(c, carry, z=z):
                zero_copy(z, c).wait()
                return carry
            lax.fori_loop(0, zn_ref[z], wbody, 0)


def _dispatch(hx, infot, tab, rows_tot, rmax):
    nblk, tb, w = hx.shape
    grid_spec = pltpu.PrefetchScalarGridSpec(
        num_scalar_prefetch=5,
        grid=(nblk,),
        in_specs=[pl.BlockSpec((1, tb, w), lambda i, *_: (i, 0, 0)),
                  pl.BlockSpec((1, 8, tb), lambda i, *_: (i, 0, 0))],
        out_specs=pl.BlockSpec(memory_space=pl.ANY),
        scratch_shapes=[pltpu.VMEM((2, rmax, w), BF16), pltpu.VMEM((MOE_SEG, w), BF16),
                        pltpu.SemaphoreType.DMA((2,)), pltpu.SemaphoreType.DMA((1,))])
    return pl.pallas_call(
        functools.partial(_dispatch_kernel, rmax=rmax),
        out_shape=jax.ShapeDtypeStruct((rows_tot, w), BF16),
        grid_spec=grid_spec,
        compiler_params=_params(("arbitrary",)),
        name="moe_dispatch",
    )(tab['loc_off'], tab['nchunk'], tab['off'], tab['zstart'], tab['znchunk'], hx, infot)


def _gffn_kernel(te_ref, nv_ref, x_ref, wg_ref, wu_ref, wd_ref, o_ref, acc_s):
    t = pl.program_id(0)
    f = pl.program_id(1)
    nf = pl.num_programs(1)
    d = o_ref.shape[1]

    @pl.when(t < nv_ref[0])
    def _():
        @pl.when(f == 0)
        def _():
            acc_s[...] = jnp.zeros_like(acc_s)

        hb = x_ref[:, :d]
        tf = wg_ref.shape[2]
        fsub = min(MOE_FSUB, tf)
        for c0 in range(0, tf, fsub):
            gt = _dot(hb, wg_ref[0, :, c0:c0 + fsub])
            up = _dot(hb, wu_ref[0, :, c0:c0 + fsub])
            act = (gt * jax.nn.sigmoid(gt) * up).astype(BF16)
            acc_s[...] += _dot(act, wd_ref[0, c0:c0 + fsub, :])

        @pl.when(f == nf - 1)
        def _():
            pieces = x_ref[:, d:].astype(F32)
            lane = lax.broadcasted_iota(jnp.int32, pieces.shape, 1)
            mine = (lane < 3 * N_EXPERTS) & (lane % N_EXPERTS == te_ref[t])
            gate = jnp.sum(jnp.where(mine, pieces, 0.0), axis=-1, keepdims=True)
            o_ref[...] = (acc_s[...] * gate).astype(BF16)

    @pl.when((t >= nv_ref[0]) & (f == nf - 1))
    def _():
        o_ref[...] = jnp.zeros_like(o_ref)


def _grouped_ffn(xs, tab, wg, wu, wd, tm, tf):
    rows, w = xs.shape
    d = wg.shape[1]
    n_tiles = rows // tm
    nf = wg.shape[2] // tf
    live = lambda t, nv: jnp.minimum(t, nv[0] - 1)
    grid_spec = pltpu.PrefetchScalarGridSpec(
        num_scalar_prefetch=2,
        grid=(n_tiles, nf),
        in_specs=[pl.BlockSpec((tm, w), lambda t, f, te, nv: (live(t, nv), 0)),
                  pl.BlockSpec((1, d, tf), lambda t, f, te, nv: (te[t], 0, f)),
                  pl.BlockSpec((1, d, tf), lambda t, f, te, nv: (te[t], 0, f)),
                  pl.BlockSpec((1, tf, d), lambda t, f, te, nv: (te[t], f, 0))],
        out_specs=pl.BlockSpec((tm, d), lambda t, f, te, nv: (t, 0)),
        scratch_shapes=[pltpu.VMEM((tm, d), F32)])
    return pl.pallas_call(
        _gffn_kernel,
        out_shape=jax.ShapeDtypeStruct((rows, d), BF16),
        grid_spec=grid_spec,
        compiler_params=_params(("arbitrary", "arbitrary")),
        name="moe_grouped_ffn",
    )(tab['tile_e'], tab['n_live'], xs, wg, wu, wd)


def _combine_kernel(lo_ref, nc_ref, off_ref, info_ref, ys_hbm, x_ref, gate_ref, gp_ref, o_ref,
                    yall, sem, *, rmax):
    i = pl.program_id(0)
    n = pl.num_programs(0)
    slot = i % 2
    tb = x_ref.shape[1]

    def chunk_copy(blk, sl, e, c):
        k = blk * N_EXPERTS + e
        src = pl.multiple_of(off_ref[k] + c * MOE_SEG, MOE_SEG)
        dst = pl.multiple_of(lo_ref[k] + c * MOE_SEG, MOE_SEG)
        return pltpu.make_async_copy(ys_hbm.at[pl.ds(src, MOE_SEG)],
                                     yall.at[sl, pl.ds(dst, MOE_SEG)], sem.at[sl])

    def fetch(blk, sl):
        yall[sl, 2 * tb:, :] = jnp.zeros((rmax - 2 * tb, yall.shape[2]), BF16)
        _segment_loops(blk, nc_ref, lambda e, c: chunk_copy(blk, sl, e, c).start())

    @pl.when(i == 0)
    def _():
        fetch(0, 0)

    @pl.when(i + 1 < n)
    def _():
        fetch(i + 1, 1 - slot)

    _segment_loops(i, nc_ref, lambda e, c: chunk_copy(i, slot, e, c).wait())
    rsub = min(MOE_CSUB, tb)
    for r0 in range(0, tb, rsub):
        info = info_ref[0, r0:r0 + rsub, :]
        loc1 = _local_rows(info[:, 0:1], info[:, 2:3], i, lo_ref)
        loc2 = _local_rows(info[:, 1:2], info[:, 3:4], i, lo_ref)
        lane = lax.broadcasted_iota(jnp.int32, (rsub, rmax), 1).astype(F32)
        sel = jnp.where((lane == loc1) | (lane == loc2), 1.0, 0.0).astype(BF16)
        o_ref[0, r0:r0 + rsub, :] = (x_ref[0, r0:r0 + rsub, :]
                                     + gate_ref[0] * _rms(_dot(sel, yall[slot]), gp_ref[...]))


def _combine(ys, info, tab, x3, gate, gp, bpb, rmax):
    nblk, tb, d = x3.shape
    grid_spec = pltpu.PrefetchScalarGridSpec(
        num_scalar_prefetch=3,
        grid=(nblk,),
        in_specs=[pl.BlockSpec((1, tb, LANES), lambda i, *_: (i, 0, 0)),
                  pl.BlockSpec(memory_space=pl.ANY),
                  pl.BlockSpec((1, tb, d), lambda i, *_: (i, 0, 0)),
                  pl.BlockSpec((1, 1, d), lambda i, *_: (i // bpb, 0, 0)),
                  pl.BlockSpec((1, d), lambda i, *_: (0, 0))],
        out_specs=pl.BlockSpec((1, tb, d), lambda i, *_: (i, 0, 0)),
        scratch_shapes=[pltpu.VMEM((2, rmax, d), BF16), pltpu.SemaphoreType.DMA((2,))])
    return pl.pallas_call(
        functools.partial(_combine_kernel, rmax=rmax),
        out_shape=jax.ShapeDtypeStruct((nblk, tb, d), F32),
        grid_spec=grid_spec,
        compiler_params=_params(("arbitrary",)),
        name="moe_combine",
    )(tab['loc_off'], tab['nchunk'], tab['off'], info, ys, x3, gate, gp)


def _moe(x, sc, sh, gpre, router, wg, wu, wd, gate, gpost, tm, tf):
    b, l, d = x.shape
    tb = min(MOE_TB, l)
    bpb = l // tb
    nblk = b * bpb
    x3 = x.reshape(nblk, tb, d)
    hx, info, infot, cnt = _router(x3, sc, sh, gpre, router, bpb)
    cnt = cnt[:, 0, :N_EXPERTS].astype(jnp.int32)
    seg_pad = N_EXPERTS * (MOE_SEG - 1)
    rmax = -(-(2 * tb + seg_pad) // 256) * 256
    n_tiles = -(-(2 * b * l + nblk * seg_pad + N_EXPERTS * (tm - 1)) // tm)
    tab = _routing_tables(cnt, tm, n_tiles)
    xs = _dispatch(hx, infot, tab, n_tiles * tm, rmax)
    ys = _grouped_ffn(xs, tab, wg, wu, wd, tm, tf)
    return _combine(ys, info, tab, x3, gate, gpost, bpb, rmax).reshape(b, l, d)


def _rope_tables(s):
    half = DIFF_DK // 2
    n_freq = half // 2
    inv = ROPE_THETA ** (-jnp.arange(n_freq, dtype=F32) / n_freq)
    t = jnp.arange(s, dtype=jnp.int32)
    rows = (t // GRID_W).astype(F32)[:, None]
    cols = (t % GRID_W).astype(F32)[:, None]
    lane = jnp.arange(LANES, dtype=jnp.int32)
    dd = lane % DIFF_DK
    pos = jnp.where((dd < half)[None, :], rows, cols)
    ang = pos * inv[dd % n_freq][None, :]
    first = ((dd % half) < n_freq)[None, :]
    cos, sin = jnp.cos(ang), jnp.sin(ang)
    return cos, jnp.where(first, -sin, 0.0), jnp.where(first, 0.0, sin)


def _dft_tables(n):
    m = 32 if n % 32 == 0 and n > 32 else 1
    j = jnp.arange(n, dtype=jnp.int32)[:, None]

    def tab(k, period):
        ang = ((j * k[None, :]) % period).astype(F32) * (2.0 * math.pi / period)
        return jnp.cos(ang)[:, :, None], jnp.sin(ang)[:, :, None]

    ca, sa = tab(jnp.arange(n // m, dtype=jnp.int32), n // m)
    cb, sb = tab(jnp.arange(m, dtype=jnp.int32), n)
    cb, sb = jnp.swapaxes(cb, 1, 2), jnp.swapaxes(sb, 1, 2)
    sc = n ** -0.5
    return (((ca * cb - sa * sb) * sc).reshape(n, n), ((sa * cb + ca * sb) * sc).reshape(n, n))


def _channel_dft():
    c, s = _dft_tables(FNET_GROUP_CH)
    eye = jnp.eye(FNET_GROUPS, dtype=F32)
    return jnp.concatenate([jnp.kron(eye, c), jnp.kron(eye, s)], axis=1).astype(BF16)


def kernel(x, c, ctx, c_ctx, ada_w, ada_b, norm_mix_pre, norm_mix_post, norm_ffn_pre, norm_ffn_post,
           w_in, w_out, diff_lq1, diff_lk1, diff_lq2, diff_lk2, diff_subln, fnet_w,
           s5_a_re, s5_a_im, s5_log_dt, s5_b_re, s5_b_im, s5_c_re, s5_c_im, s5_d, s5_w_glu,
           ffn_w_gate, ffn_w_up, ffn_w_down, moe_router, moe_w_gate, moe_w_up, moe_w_down):
    b, s, d = x.shape
    lc = ctx.shape[1]
    depth = ada_w.shape[0]

    cc = jnp.zeros((MOD_ROWS, d), F32).at[:b].set(c).at[b].set(c_ctx)
    mod = _modulation(cc, ada_w, ada_b)

    rope_tabs = _rope_tables(s)
    cs64 = _channel_dft()
    dft = {n: tuple(t.astype(BF16) for t in _dft_tables(n)) for n in (s, lc)}

    xc = ctx
    for l in range(depth):
        need_ctx = l < depth - 1
        lam_init = 0.8 - 0.6 * math.exp(-0.3 * l)
        m_lat = [mod[l, :b, i * d:(i + 1) * d].reshape(b, 1, d) for i in range(6)]
        m_ctx = [jnp.broadcast_to(mod[l, b, i * d:(i + 1) * d].reshape(1, 1, d), (b, 1, d))
                 for i in range(6)]
        row = lambda v: v.reshape(1, -1).astype(F32)
        w_in_bf = w_in[l].astype(BF16)
        w_out_bf = w_out[l].astype(BF16)
        fw_bf = fnet_w[l].astype(BF16)
        wglu_bf = s5_w_glu[l].astype(BF16)
        lams = (row(diff_lq1[l]), row(diff_lk1[l]), row(diff_lq2[l]), row(diff_lk2[l]))
        subln = row(diff_subln[l])

        q, k, v, g1, g2, u = _inproj(x, m_lat[1], m_lat[0], row(norm_mix_pre[l]), w_in_bf, cs64,
                                     rope_tabs, min(INPROJ_TL, s))
        qc, kc, vc, g1c, g2c, uc = _inproj(xc, m_ctx[1], m_ctx[0], row(norm_mix_pre[l]), w_in_bf,
                                           cs64, None, lc)
        a_lat = _attention(q, [(kc, vc), (k, v)], lams, subln, lam_init, min(ATTN_TQ, s))
        f_lat = _fnet_dft(*dft[s], g1, g2, fw_bf, b, min(512, s), 2)

        mats = [_s5_matrices(s5_a_re[l, dr], s5_a_im[l, dr], s5_log_dt[l, dr], s5_b_re[l, dr],
                             s5_b_im[l, dr], s5_c_re[l, dr], s5_c_im[l, dr]) for dr in (0, 1)]
        mats = (mats[0][0], mats[1][0], mats[0][1], mats[1][1], mats[0][2], mats[1][2])
        zero_state = jnp.zeros((b, 2 * S5_LANES), F32)
        ycf, ycr, hcf, hcr = _s5_scan(uc, mats, zero_state, zero_state, b, S5_TC)
        yf, yr, _, _ = _s5_scan(u, mats, hcf, hcr, b, S5_TC)

        x = _outproj(a_lat, f_lat, yf, yr, u, row(s5_d[l]), wglu_bf, w_out_bf,
                     x, m_lat[2], row(norm_mix_post[l]), min(512, s))
        if need_ctx:
            a_ctx = _attention(qc, [(kc, vc)], lams, subln, lam_init, lc)
            f_ctx = _fnet_dft(*dft[lc], g1c, g2c, fw_bf, b, lc, 2)
            xc = _outproj(a_ctx, f_ctx, ycf, ycr, uc, row(s5_d[l]), wglu_bf,
                          w_out_bf, xc, m_ctx[2], row(norm_mix_post[l]), lc)

        i = l // 2
        gpre, gpost = row(norm_ffn_pre[l]), row(norm_ffn_post[l])
        xc1 = xc.reshape(1, b * lc, d)
        if l % 2 == 0:
            wg, wu, wd = (ffn_w_gate[i].astype(BF16), ffn_w_up[i].astype(BF16),
                          ffn_w_down[i].astype(BF16))
            tf = FFN_DENSE_TF
            x = _ffn(x, m_lat[4], m_lat[3], gpre, wg, wu, wd, m_lat[5], gpost, min(1024, s), tf)
            if need_ctx:
                xc1 = _ffn(xc1, m_ctx[4][:1], m_ctx[3][:1], gpre, wg, wu, wd, m_ctx[5][:1], gpost,
                           min(1024, b * lc), tf)
        else:
            wg, wu, wd = (moe_w_gate[i].astype(BF16), moe_w_up[i].astype(BF16),
                          moe_w_down[i].astype(BF16))
            r = jnp.zeros((d, LANES), F32).at[:, :N_EXPERTS].set(moe_router[i])
            r_hi = r.astype(BF16)
            router = jnp.stack([r_hi, (r - r_hi.astype(F32)).astype(BF16)])
            x = _moe(x, m_lat[4], m_lat[3], gpre, router, wg, wu, wd, m_lat[5], gpost,
                     MOE_TM, MOE_TF)
            if need_ctx:
                xc1 = _moe(xc1, m_ctx[4][:1], m_ctx[3][:1], gpre, router, wg, wu, wd, m_ctx[5][:1],
                           gpost, MOE_TM, MOE_TF)
        xc = xc1.reshape(b, lc, d)
    return x
```

```python
import functools
import math

import jax
import jax.numpy as jnp
from jax import lax
from jax.experimental import pallas as pl
from jax.experimental.pallas import tpu as pltpu

F32 = jnp.float32
BF16 = jnp.bfloat16

D_MODEL = 1024
DEPTH = 2
GRID_W = 64
EPS = 1e-6
DIFF_HEADS = 4
DIFF_DK = 64
DIFF_DV = 2 * DIFF_DK
DIFF_WIDTH = DIFF_HEADS * DIFF_DV
DIFF_QK_WIDTH = DIFF_HEADS * 2 * DIFF_DK
ROPE_THETA = 10000.0
FNET_GROUPS = 4
FNET_GROUP_CH = 64
FNET_WIDTH = FNET_GROUPS * FNET_GROUP_CH
S5_CH = 16
S5_GROUPS = 16
S5_STATE = 64
S5_WIDTH = S5_GROUPS * S5_CH
S5_LANES = S5_GROUPS * S5_STATE
IN_WIDTH = 2 * DIFF_QK_WIDTH + DIFF_WIDTH + FNET_WIDTH + S5_WIDTH
N_EXPERTS = 8
LANES = 128
LOG2E = math.log2(math.e)
MOD_ROWS = 24
FFN_DENSE_TF = 1408
INPROJ_TL = 1024
INPROJ_RSUB = 256
ATTN_TQ = 2048
ATTN_RSUB = 128
S5_TC = 32
MOE_TB = 512
MOE_SEG = 16
MOE_RSUB = 640
MOE_CSUB = 256
MOE_TM = 1024
MOE_TF = 1792
MOE_FSUB = 896

VMEM_LIMIT = 56 * 1024 * 1024


def _params(sem):
    return pltpu.CompilerParams(dimension_semantics=sem, vmem_limit_bytes=VMEM_LIMIT)


def _rms(x, g):
    return x * lax.rsqrt(jnp.mean(x * x, axis=-1, keepdims=True) + EPS) * g


def _dot(a, b):
    return jnp.dot(a, b, preferred_element_type=F32)


def _dot_row_halves(a, b):
    half = a.shape[0] // 2
    return jnp.concatenate([_dot(a[:half, :], b), _dot(a[half:, :], b)], axis=0)


def _mod_kernel(c_ref, w_ref, b_ref, o_ref):
    c = c_ref[...]
    sc = c * jax.nn.sigmoid(c)
    o_ref[0] = jnp.dot(sc, w_ref[0], preferred_element_type=F32,
                       precision=lax.Precision.HIGHEST) + b_ref[0]


def _modulation(cc, ada_w, ada_b):
    depth, d, n = ada_w.shape
    tn = 1536
    return pl.pallas_call(
        _mod_kernel,
        out_shape=jax.ShapeDtypeStruct((depth, MOD_ROWS, n), F32),
        grid=(depth, n // tn),
        in_specs=[pl.BlockSpec((MOD_ROWS, d), lambda l, j: (0, 0)),
                  pl.BlockSpec((1, d, tn), lambda l, j: (l, 0, j)),
                  pl.BlockSpec((1, 1, tn), lambda l, j: (l, 0, j))],
        out_specs=pl.BlockSpec((1, MOD_ROWS, tn), lambda l, j: (l, 0, j)),
        compiler_params=_params(("parallel", "parallel")),
        name="adaln_mod",
    )(cc, ada_w, ada_b.reshape(depth, 1, n))


def _inproj_kernel(*refs, rope):
    if rope:
        (x_ref, sc_ref, sh_ref, g_ref, w_ref, cs_ref, cos_ref, sina_ref, sinb_ref,
         q_ref, k_ref, v_ref, g1_ref, g2_ref, u_ref) = refs
    else:
        (x_ref, sc_ref, sh_ref, g_ref, w_ref, cs_ref,
         q_ref, k_ref, v_ref, g1_ref, g2_ref, u_ref) = refs
    tl = x_ref.shape[1]
    rsub = min(INPROJ_RSUB, tl)
    for r0 in range(0, tl, rsub):
        rows = slice(r0, r0 + rsub)
        h = _rms(x_ref[0, rows, :], g_ref[...]) * (1.0 + sc_ref[0]) + sh_ref[0]
        hb = h.astype(BF16)

        def proj(lo, hi):
            return _dot(hb, w_ref[:, lo:hi])

        def rotate(t):
            outs = []
            for j in range(t.shape[1] // LANES):
                tb = t[:, j * LANES:(j + 1) * LANES]
                outs.append(tb * cos_ref[rows, :]
                            + pltpu.roll(tb, LANES - 16, axis=1) * sina_ref[rows, :]
                            + pltpu.roll(tb, 16, axis=1) * sinb_ref[rows, :])
            return jnp.concatenate(outs, axis=1)

        q = proj(0, DIFF_QK_WIDTH)
        k = proj(DIFF_QK_WIDTH, 2 * DIFF_QK_WIDTH)
        if rope:
            q = rotate(q)
            k = rotate(k)
        q_ref[0, rows, :] = (q * (DIFF_DK ** -0.5 * LOG2E)).astype(BF16)
        k_ref[0, rows, :] = k.astype(BF16)
        o = 2 * DIFF_QK_WIDTH
        v_ref[0, rows, :] = proj(o, o + DIFF_WIDTH).astype(BF16)
        o += DIFF_WIDTH
        f = proj(o, o + FNET_WIDTH).astype(BF16)
        g12 = _dot(f, cs_ref[...])
        g1_ref[rows, :] = g12[:, :FNET_WIDTH].astype(BF16)
        g2_ref[rows, :] = g12[:, FNET_WIDTH:].astype(BF16)
        u_ref[rows, :] = proj(o + FNET_WIDTH, IN_WIDTH)


def _inproj(x, sc, sh, g, w_bf, cs64, rope_tabs, tl):
    b, l, d = x.shape
    rope = rope_tabs is not None
    in_specs = [pl.BlockSpec((1, tl, d), lambda i, bb: (bb, i, 0)),
                pl.BlockSpec((1, 1, d), lambda i, bb: (bb, 0, 0)),
                pl.BlockSpec((1, 1, d), lambda i, bb: (bb, 0, 0)),
                pl.BlockSpec((1, d), lambda i, bb: (0, 0)),
                pl.BlockSpec((d, IN_WIDTH), lambda i, bb: (0, 0)),
                pl.BlockSpec((FNET_WIDTH, 2 * FNET_WIDTH), lambda i, bb: (0, 0))]
    args = [x, sc, sh, g, w_bf, cs64]
    if rope:
        in_specs += [pl.BlockSpec((tl, LANES), lambda i, bb: (i, 0))] * 3
        args += list(rope_tabs)
    tok = lambda w: pl.BlockSpec((1, tl, w), lambda i, bb: (bb, i, 0))
    tb = pl.BlockSpec((tl, FNET_WIDTH), lambda i, bb: (i, bb))
    return pl.pallas_call(
        functools.partial(_inproj_kernel, rope=rope),
        out_shape=(jax.ShapeDtypeStruct((b, l, DIFF_QK_WIDTH), BF16),
                   jax.ShapeDtypeStruct((b, l, DIFF_QK_WIDTH), BF16),
                   jax.ShapeDtypeStruct((b, l, DIFF_WIDTH), BF16),
                   jax.ShapeDtypeStruct((l, b * FNET_WIDTH), BF16),
                   jax.ShapeDtypeStruct((l, b * FNET_WIDTH), BF16),
                   jax.ShapeDtypeStruct((l, b * S5_WIDTH), F32)),
        grid=(l // tl, b),
        in_specs=in_specs,
        out_specs=(tok(DIFF_QK_WIDTH), tok(DIFF_QK_WIDTH), tok(DIFF_WIDTH), tb, tb, tb),
        compiler_params=_params(("parallel", "parallel")),
        name="inproj_rope" if rope else "inproj",
    )(*args)


def _attn_kernel(*refs, n_src, lam_init, rsub):
    q_ref = refs[0]
    kv = refs[1:1 + 2 * n_src]
    lq1, lk1, lq2, lk2, sub_ref, o_ref, k1_s, k2_s, v_s = refs[1 + 2 * n_src:]

    @pl.when(pl.program_id(2) == 0)
    def _():
        off = 0
        for s in range(n_src):
            kk = kv[2 * s][0]
            n = kk.shape[0]
            k1_s[off:off + n, :] = kk[:, :DIFF_DK]
            k2_s[off:off + n, :] = kk[:, DIFF_DK:]
            v_s[off:off + n, :DIFF_DV] = kv[2 * s + 1][0]
            off += n
        v_s[:, DIFF_DV:] = jnp.ones((v_s.shape[0], DIFF_DV), BF16)

    lam = (jnp.exp(jnp.sum(lq1[...] * lk1[...], axis=-1, keepdims=True))
           - jnp.exp(jnp.sum(lq2[...] * lk2[...], axis=-1, keepdims=True)) + lam_init)
    def attend(qj, k_s):
        s = lax.dot_general(qj, k_s[...], (((1,), (1,)), ((), ())), preferred_element_type=F32)
        p = jnp.exp2(s - jnp.max(s, axis=-1, keepdims=True)).astype(BF16)
        ol = _dot(p, v_s[...])
        return ol[:, :DIFF_DV] / ol[:, DIFF_DV:DIFF_DV + 1]

    for r0 in range(0, q_ref.shape[1], rsub):
        q = q_ref[0, r0:r0 + rsub, :]
        o = attend(q[:, :DIFF_DK], k1_s) - lam * attend(q[:, DIFF_DK:], k2_s)
        o_ref[0, r0:r0 + rsub, :] = (_rms(o, sub_ref[...]) * (1.0 - lam_init)).astype(BF16)


def _attention(q, kv_srcs, lams, subln, lam_init, tq):
    b, lq, _ = q.shape
    n_src = len(kv_srcs)
    lk = sum(k.shape[1] for k, _ in kv_srcs)
    in_specs = [pl.BlockSpec((1, tq, DIFF_DV), lambda bb, h, i: (bb, i, h))]
    args = [q]
    for k, v in kv_srcs:
        spec = pl.BlockSpec((1, k.shape[1], DIFF_DV), lambda bb, h, i: (bb, 0, h))
        in_specs += [spec, spec]
        args += [k, v]
    in_specs += [pl.BlockSpec((1, DIFF_DK), lambda bb, h, i: (0, 0))] * 4
    in_specs += [pl.BlockSpec((1, DIFF_DV), lambda bb, h, i: (0, 0))]
    args += list(lams) + [subln]
    return pl.pallas_call(
        functools.partial(_attn_kernel, n_src=n_src, lam_init=lam_init, rsub=min(ATTN_RSUB, tq)),
        out_shape=jax.ShapeDtypeStruct((b, lq, DIFF_WIDTH), BF16),
        grid=(b, DIFF_HEADS, lq // tq),
        in_specs=in_specs,
        out_specs=pl.BlockSpec((1, tq, DIFF_DV), lambda bb, h, i: (bb, i, h)),
        scratch_shapes=[pltpu.VMEM((lk, DIFF_DK), BF16), pltpu.VMEM((lk, DIFF_DK), BF16),
                        pltpu.VMEM((lk, 2 * DIFF_DV), BF16)],
        compiler_params=_params(("parallel", "parallel", "arbitrary")),
        name="diff_attn_%d" % n_src,
    )(*args)


def _dft_kernel(c_ref, s_ref, g1_ref, g2_ref, w_ref, o_ref):
    z = _dot(c_ref[...], g1_ref[...]) - _dot(s_ref[...], g2_ref[...])
    for j in range(o_ref.shape[0]):
        zj = z[:, j * FNET_WIDTH:(j + 1) * FNET_WIDTH].astype(BF16)
        o_ref[j] = _dot(zj, w_ref[...]).astype(BF16)


def _fnet_dft(cosm, sinm, g1, g2, w_bf, b, tm, nb):
    l = cosm.shape[0]
    tn = nb * FNET_WIDTH
    return pl.pallas_call(
        _dft_kernel,
        out_shape=jax.ShapeDtypeStruct((b, l, FNET_WIDTH), BF16),
        grid=(l // tm, b // nb),
        in_specs=[pl.BlockSpec((tm, l), lambda i, j: (i, 0)),
                  pl.BlockSpec((tm, l), lambda i, j: (i, 0)),
                  pl.BlockSpec((l, tn), lambda i, j: (0, j)),
                  pl.BlockSpec((l, tn), lambda i, j: (0, j)),
                  pl.BlockSpec((FNET_WIDTH, FNET_WIDTH), lambda i, j: (0, 0))],
        out_specs=pl.BlockSpec((nb, tm, FNET_WIDTH), lambda i, j: (j, i, 0)),
        compiler_params=_params(("parallel", "parallel")),
        name="fnet_dft",
    )(cosm, sinm, g1, g2, w_bf)


def _s5_kernel(uf_ref, ur_ref, perm_ref, permt_ref, bf_ref, br_ref, lf_ref, lr_ref, cf_ref, cr_ref,
               h0f_ref, h0r_ref, yf_ref, yr_ref, hef_ref, her_ref, hs_f, hs_r, *, tc, nb):
    j = pl.program_id(0)

    @pl.when(j == 0)
    def _():
        hs_f[...] = h0f_ref[...]
        hs_r[...] = h0r_ref[...]

    def drive(u_ref, b_ref):
        u_bt = jnp.concatenate(
            [u_ref[:, bb * S5_WIDTH:(bb + 1) * S5_WIDTH] for bb in range(nb)], axis=0).astype(BF16)
        u_tb = _dot_row_halves(perm_ref, u_bt).astype(BF16)
        return _dot(u_tb, b_ref[...])

    def scan(drv, l_ref, hs, reverse):
        hr, hi = hs[:, :S5_LANES], hs[:, S5_LANES:]
        re_rows, im_rows = [None] * tc, [None] * tc
        for t in (range(tc - 1, -1, -1) if reverse else range(tc)):
            lre = l_ref[:, :S5_LANES]
            lim = l_ref[:, S5_LANES:]
            d = drv[t * nb:(t + 1) * nb, :]
            hr, hi = (lre * hr - lim * hi + d[:, :S5_LANES], lre * hi + lim * hr + d[:, S5_LANES:])
            re_rows[t] = hr.astype(BF16)
            im_rows[t] = hi.astype(BF16)
        hs[:, :S5_LANES] = hr
        hs[:, S5_LANES:] = hi
        return jnp.concatenate(re_rows, axis=0), jnp.concatenate(im_rows, axis=0)

    def readout(h, c_ref, y_ref):
        half = h[0].shape[0] // 2
        y = jnp.concatenate(
            [_dot(h[0][r0:r0 + half], c_ref[:S5_LANES, :]) + _dot(h[1][r0:r0 + half], c_ref[S5_LANES:, :])
             for r0 in (0, half)], axis=0)
        y_hi = y.astype(BF16)
        y_lo = (y - y_hi.astype(F32)).astype(BF16)
        y_bt = _dot_row_halves(permt_ref, y_hi) + _dot_row_halves(permt_ref, y_lo)
        for bb in range(nb):
            y_ref[:, bb * S5_WIDTH:(bb + 1) * S5_WIDTH] = y_bt[bb * tc:(bb + 1) * tc, :]

    drv_f = drive(uf_ref, bf_ref)
    drv_r = drive(ur_ref, br_ref)
    readout(scan(drv_f, lf_ref, hs_f, False), cf_ref, yf_ref)
    readout(scan(drv_r, lr_ref, hs_r, True), cr_ref, yr_ref)

    @pl.when(j == pl.num_programs(0) - 1)
    def _():
        hef_ref[...] = hs_f[...]
        her_ref[...] = hs_r[...]


def _s5_scan(u, mats, h0f, h0r, nb, tc):
    l = u.shape[0]
    n = l // tc
    r = tc * nb
    bmf, bmr, lf, lr, cmf, cmr = mats
    lf, lr = (jnp.broadcast_to(v, (nb, v.shape[1])) for v in (lf, lr))
    rows = jnp.arange(r, dtype=jnp.int32)
    perm = (rows[None, :] == ((rows % nb) * tc + rows // nb)[:, None]).astype(BF16)
    permt = perm.T
    full = lambda a: pl.BlockSpec(a.shape, lambda j: (0,) * a.ndim)
    fwd = pl.BlockSpec((tc, nb * S5_WIDTH), lambda j: (j, 0))
    rev = pl.BlockSpec((tc, nb * S5_WIDTH), lambda j: (n - 1 - j, 0))
    st = jax.ShapeDtypeStruct((nb, 2 * S5_LANES), F32)
    return pl.pallas_call(
        functools.partial(_s5_kernel, tc=tc, nb=nb),
        out_shape=(jax.ShapeDtypeStruct(u.shape, F32), jax.ShapeDtypeStruct(u.shape, F32), st, st),
        grid=(n,),
        in_specs=[fwd, rev, full(perm), full(permt), full(bmf), full(bmr), full(lf), full(lr),
                  full(cmf), full(cmr), full(h0f), full(h0r)],
        out_specs=(fwd, rev, full(h0f), full(h0r)),
        scratch_shapes=[pltpu.VMEM((nb, 2 * S5_LANES), F32), pltpu.VMEM((nb, 2 * S5_LANES), F32)],
        compiler_params=_params(("arbitrary",)),
        name="s5_scan",
    )(u, u, perm, permt, bmf, bmr, lf, lr, cmf, cmr, h0f, h0r)


def _s5_matrices(a_re, a_im, log_dt, b_re, b_im, c_re, c_im):
    dt = jnp.exp(log_dt)[:, None]
    mag = jnp.exp(a_re * dt)
    lr, li = mag * jnp.cos(a_im * dt), mag * jnp.sin(a_im * dt)
    nr, ni = lr - 1.0, li
    den = a_re * a_re + a_im * a_im
    cr = (nr * a_re + ni * a_im) / den
    ci = (ni * a_re - nr * a_im) / den
    bbr = cr[..., None] * b_re - ci[..., None] * b_im
    bbi = cr[..., None] * b_im + ci[..., None] * b_re
    eye = jnp.eye(S5_GROUPS, dtype=F32)

    def drive_mat(bb):
        return jnp.einsum('gpc,gh->gchp', bb, eye).reshape(S5_WIDTH, S5_LANES)

    def read_mat(cc):
        return jnp.einsum('gcp,gh->gphc', cc, eye).reshape(S5_LANES, S5_WIDTH)

    bm = jnp.concatenate([drive_mat(bbr), drive_mat(bbi)], axis=1).astype(BF16)
    cm = jnp.concatenate([read_mat(c_re), -read_mat(c_im)], axis=0).astype(BF16)
    lam = jnp.concatenate([lr.reshape(1, S5_LANES), li.reshape(1, S5_LANES)], axis=1)
    return bm, lam, cm


def _outproj_kernel(a_ref, f_ref, yf_ref, yr_ref, u_ref, d_ref, wglu_ref, wa_ref, wf_ref, ws_ref,
                    x_ref, gate_ref, g_ref, o_ref):
    yy = yf_ref[...] + yr_ref[...] + d_ref[...] * u_ref[...]
    yy = jax.nn.gelu(yy)
    s = yy * jax.nn.sigmoid(_dot(yy.astype(BF16), wglu_ref[...]))
    mix = (_dot(a_ref[0], wa_ref[...]) + _dot(f_ref[0], wf_ref[...])
           + _dot(s.astype(BF16), ws_ref[...]))
    o_ref[0] = x_ref[0] + gate_ref[0] * _rms(mix, g_ref[...])


def _outproj(a, fo, yf, yr, u, d, wglu, w_out, x, gate, g, tl):
    b, l, dm = x.shape
    wa, wf, ws = (w_out[:DIFF_WIDTH], w_out[DIFF_WIDTH:DIFF_WIDTH + FNET_WIDTH],
                  w_out[DIFF_WIDTH + FNET_WIDTH:])
    tok = lambda w: pl.BlockSpec((1, tl, w), lambda i, bb: (bb, i, 0))
    tb = pl.BlockSpec((tl, S5_WIDTH), lambda i, bb: (i, bb))
    full = lambda arr: pl.BlockSpec(arr.shape, lambda i, bb: (0,) * arr.ndim)
    return pl.pallas_call(
        _outproj_kernel,
        out_shape=jax.ShapeDtypeStruct((b, l, dm), F32),
        grid=(l // tl, b),
        in_specs=[tok(DIFF_WIDTH), tok(FNET_WIDTH), tb, tb, tb, full(d), full(wglu),
                  full(wa), full(wf), full(ws), tok(dm),
                  pl.BlockSpec((1, 1, dm), lambda i, bb: (bb, 0, 0)), full(g)],
        out_specs=tok(dm),
        compiler_params=_params(("parallel", "parallel")),
        name="outproj",
    )(a, fo, yf, yr, u, d, wglu, wa, wf, ws, x, gate, g)


def _ffn_kernel(x_ref, sc_ref, sh_ref, g_ref, wg_ref, wu_ref, wd_ref, gate_ref, gp_ref,
                o_ref, h_s, acc_s):
    f = pl.program_id(2)

    @pl.when(f == 0)
    def _():
        h = _rms(x_ref[0], g_ref[...]) * (1.0 + sc_ref[0]) + sh_ref[0]
        h_s[...] = h.astype(BF16)
        acc_s[...] = jnp.zeros_like(acc_s)

    hb = h_s[...]
    gt = _dot(hb, wg_ref[...])
    up = _dot(hb, wu_ref[...])
    act = (gt * jax.nn.sigmoid(gt) * up).astype(BF16)
    acc_s[...] += _dot(act, wd_ref[...])

    @pl.when(f == pl.num_programs(2) - 1)
    def _():
        o_ref[0] = x_ref[0] + gate_ref[0] * _rms(acc_s[...], gp_ref[...])


def _ffn(x, sc, sh, g, wg, wu, wd, gate, gp, tm, tf):
    b, l, d = x.shape
    fdim = wg.shape[1]
    tok = pl.BlockSpec((1, tm, d), lambda bb, i, f: (bb, i, 0))
    modv = pl.BlockSpec((1, 1, d), lambda bb, i, f: (bb, 0, 0))
    vec = pl.BlockSpec((1, d), lambda bb, i, f: (0, 0))
    return pl.pallas_call(
        _ffn_kernel,
        out_shape=jax.ShapeDtypeStruct((b, l, d), F32),
        grid=(b, l // tm, fdim // tf),
        in_specs=[tok, modv, modv, vec,
                  pl.BlockSpec((d, tf), lambda bb, i, f: (0, f)),
                  pl.BlockSpec((d, tf), lambda bb, i, f: (0, f)),
                  pl.BlockSpec((tf, d), lambda bb, i, f: (f, 0)),
                  modv, vec],
        out_specs=tok,
        scratch_shapes=[pltpu.VMEM((tm, d), BF16), pltpu.VMEM((tm, d), F32)],
        compiler_params=_params(("parallel", "parallel", "arbitrary")),
        name="dense_ffn",
    )(x, sc, sh, g, wg, wu, wd, gate, gp)


def _router_kernel(x_ref, sc_ref, sh_ref, g_ref, r_ref, tri_ref, hx_ref, info_ref, infot_ref, cnt_ref):
    d = x_ref.shape[2]
    h = _rms(x_ref[0], g_ref[...]) * (1.0 + sc_ref[0]) + sh_ref[0]
    hi = h.astype(BF16)
    lo = (h - hi.astype(F32)).astype(BF16)
    logits = (_dot_row_halves(hi, r_ref[0]) + _dot_row_halves(hi, r_ref[1])
              + _dot_row_halves(lo, r_ref[0]))
    lane = lax.broadcasted_iota(jnp.int32, logits.shape, 1).astype(F32)
    neg = jnp.float32(-jnp.inf)
    logits = jnp.where(lane < N_EXPERTS, logits, neg)
    m1 = jnp.max(logits, axis=-1, keepdims=True)
    i1 = jnp.min(jnp.where(logits == m1, lane, float(LANES)), axis=-1, keepdims=True)
    rest = jnp.where(lane == i1, neg, logits)
    m2 = jnp.max(rest, axis=-1, keepdims=True)
    i2 = jnp.min(jnp.where(rest == m2, lane, float(LANES)), axis=-1, keepdims=True)
    e2 = jnp.exp(m2 - m1)
    w1 = 1.0 / (1.0 + e2)
    oh1 = lane == i1
    oh2 = lane == i2
    member = jnp.where(oh1 | oh2, 1.0, 0.0)
    ranks = _dot_row_halves(tri_ref, member.astype(BF16))
    pos1 = jnp.sum(jnp.where(oh1, ranks, 0.0), axis=-1, keepdims=True)
    pos2 = jnp.sum(jnp.where(oh2, ranks, 0.0), axis=-1, keepdims=True)
    cnt_ref[0] = jnp.broadcast_to(jnp.sum(member, axis=0, keepdims=True), cnt_ref.shape[1:])
    gates = jnp.where(oh1, w1, 0.0) + jnp.where(oh2, e2 * w1, 0.0)
    g_hi = gates.astype(BF16).astype(F32)
    g_mid = (gates - g_hi).astype(BF16).astype(F32)
    g_lo = gates - g_hi - g_mid
    pieces = g_hi + pltpu.roll(g_mid, N_EXPERTS, axis=1) + pltpu.roll(g_lo, 2 * N_EXPERTS, axis=1)
    hx_ref[0, :, :d] = hi
    hx_ref[0, :, d:] = pieces.astype(BF16)
    info = jnp.where(lane == 0, i1, jnp.where(lane == 1, i2, jnp.where(
        lane == 2, pos1, jnp.where(lane == 3, pos2, 0.0))))
    info_ref[0] = info
    infot_ref[0] = info.T[:8, :]


def _router(x3, sc, sh, g, router, bpb):
    nblk, tb, d = x3.shape
    tok = lambda w: pl.BlockSpec((1, tb, w), lambda i: (i, 0, 0))
    modv = pl.BlockSpec((1, 1, d), lambda i: (i // bpb, 0, 0))
    r = jnp.arange(tb, dtype=jnp.int32)
    tri = (r[None, :] < r[:, None]).astype(BF16)
    return pl.pallas_call(
        _router_kernel,
        out_shape=(jax.ShapeDtypeStruct((nblk, tb, d + LANES), BF16),
                   jax.ShapeDtypeStruct((nblk, tb, LANES), F32),
                   jax.ShapeDtypeStruct((nblk, 8, tb), F32),
                   jax.ShapeDtypeStruct((nblk, 8, LANES), F32)),
        grid=(nblk,),
        in_specs=[tok(d), modv, modv, pl.BlockSpec((1, d), lambda i: (0, 0)),
                  pl.BlockSpec(router.shape, lambda i: (0, 0, 0)),
                  pl.BlockSpec((tb, tb), lambda i: (0, 0))],
        out_specs=(tok(d + LANES), tok(LANES), pl.BlockSpec((1, 8, tb), lambda i: (i, 0, 0)),
                   pl.BlockSpec((1, 8, LANES), lambda i: (i, 0, 0))),
        compiler_params=_params(("parallel",)),
        name="moe_router",
    )(x3, sc, sh, g, router, tri)


def _routing_tables(cnt, tm, n_tiles):
    i32 = jnp.int32
    pcnt = ((cnt + MOE_SEG - 1) // MOE_SEG) * MOE_SEG
    loc_off = jnp.cumsum(pcnt, axis=1) - pcnt
    used = jnp.sum(pcnt, axis=0)
    region = ((used + tm - 1) // tm) * tm
    ends = jnp.cumsum(region)
    base = ends - region
    off = base[None, :] + jnp.cumsum(pcnt, axis=0) - pcnt
    tile_row0 = jnp.arange(n_tiles, dtype=i32) * tm
    tile_e = jnp.minimum(jnp.sum((ends[None, :] <= tile_row0[:, None]).astype(i32), axis=1),
                         N_EXPERTS - 1)
    flat = lambda a: a.reshape(-1).astype(i32)
    zstart = jnp.concatenate([base + used, ends[-1:]])
    zrows = jnp.concatenate([region - used, n_tiles * tm - ends[-1:]])
    return dict(loc_off=flat(loc_off), nchunk=flat(pcnt // MOE_SEG), off=flat(off),
                zstart=flat(zstart), znchunk=flat(zrows // MOE_SEG),
                tile_e=tile_e.astype(i32), n_live=(ends[-1:] // tm).astype(i32))


def _segment_loops(blk, nchunk_ref, fn):
    for e in range(N_EXPERTS):
        def body(c, carry, e=e):
            fn(e, c)
            return carry
        lax.fori_loop(0, nchunk_ref[blk * N_EXPERTS + e], body, 0)


def _local_rows(idx, pos, blk, lo_ref):
    loc = pos
    for e in range(N_EXPERTS):
        loc = loc + jnp.where(idx == float(e), lo_ref[blk * N_EXPERTS + e].astype(F32), 0.0)
    return loc


def _dispatch_kernel(lo_ref, nc_ref, off_ref, zs_ref, zn_ref, hx_ref, it_ref, xs_hbm,
                     xall, zbuf, sem, zsem, *, rmax):
    i = pl.program_id(0)
    n = pl.num_programs(0)
    slot = i % 2

    def chunk_copy(blk, sl, e, c):
        k = blk * N_EXPERTS + e
        src = pl.multiple_of(lo_ref[k] + c * MOE_SEG, MOE_SEG)
        dst = pl.multiple_of(off_ref[k] + c * MOE_SEG, MOE_SEG)
        return pltpu.make_async_copy(xall.at[sl, pl.ds(src, MOE_SEG)],
                                     xs_hbm.at[pl.ds(dst, MOE_SEG)], sem.at[sl])

    def zero_copy(e, c):
        dst = pl.multiple_of(zs_ref[e] + c * MOE_SEG, MOE_SEG)
        return pltpu.make_async_copy(zbuf, xs_hbm.at[pl.ds(dst, MOE_SEG)], zsem.at[0])

    @pl.when(i >= 2)
    def _():
        _segment_loops(i - 2, nc_ref, lambda e, c: chunk_copy(i - 2, slot, e, c).wait())

    it = it_ref[0]
    loc1 = _local_rows(it[0:1, :], it[2:3, :], i, lo_ref)
    loc2 = _local_rows(it[1:2, :], it[3:4, :], i, lo_ref)
    hx = hx_ref[0]
    rsub = MOE_RSUB if rmax % MOE_RSUB == 0 else rmax
    for r0 in range(0, rmax, rsub):
        row = (lax.broadcasted_iota(jnp.int32, (rsub, it.shape[1]), 0) + r0).astype(F32)
        sel = jnp.where((row == loc1) | (row == loc2), 1.0, 0.0).astype(BF16)
        xall[slot, r0:r0 + rsub, :] = _dot(sel, hx).astype(BF16)
    _segment_loops(i, nc_ref, lambda e, c: chunk_copy(i, slot, e, c).start())

    @pl.when(i == n - 1)
    def _():
        zbuf[...] = jnp.zeros_like(zbuf)
        for z in range(N_EXPERTS + 1):
            def zbody(c, carry, z=z):
                zero_copy(z, c).start()
                return carry
            lax.fori_loop(0, zn_ref[z], zbody, 0)

        @pl.when(n >= 2)
        def _():
            _segment_loops(i - 1, nc_ref, lambda e, c: chunk_copy(i - 1, 1 - slot, e, c).wait())
        _segment_loops(i, nc_ref, lambda e, c: chunk_copy(i, slot, e, c).wait())
        for z in range(N_EXPERTS + 1):
            def wbody(c, carry, z=z):
                zero_copy(z, c).wait()
                return carry
            lax.fori_loop(0, zn_ref[z], wbody, 0)


def _dispatch(hx, infot, tab, rows_tot, rmax):
    nblk, tb, w = hx.shape
    grid_spec = pltpu.PrefetchScalarGridSpec(
        num_scalar_prefetch=5,
        grid=(nblk,),
        in_specs=[pl.BlockSpec((1, tb, w), lambda i, *_: (i, 0, 0)),
                  pl.BlockSpec((1, 8, tb), lambda i, *_: (i, 0, 0))],
        out_specs=pl.BlockSpec(memory_space=pl.ANY),
        scratch_shapes=[pltpu.VMEM((2, rmax, w), BF16), pltpu.VMEM((MOE_SEG, w), BF16),
                        pltpu.SemaphoreType.DMA((2,)), pltpu.SemaphoreType.DMA((1,))])
    return pl.pallas_call(
        functools.partial(_dispatch_kernel, rmax=rmax),
        out_shape=jax.ShapeDtypeStruct((rows_tot, w), BF16),
        grid_spec=grid_spec,
        compiler_params=_params(("arbitrary",)),
        name="moe_dispatch",
    )(tab['loc_off'], tab['nchunk'], tab['off'], tab['zstart'], tab['znchunk'], hx, infot)


def _gffn_kernel(te_ref, nv_ref, x_ref, wg_ref, wu_ref, wd_ref, o_ref, acc_s):
    t = pl.program_id(0)
    f = pl.program_id(1)
    nf = pl.num_programs(1)
    d = o_ref.shape[1]

    @pl.when(t < nv_ref[0])
    def _():
        @pl.when(f == 0)
        def _():
            acc_s[...] = jnp.zeros_like(acc_s)

        hb = x_ref[:, :d]
        tf = wg_ref.shape[2]
        fsub = min(MOE_FSUB, tf)
        for c0 in range(0, tf, fsub):
            gt = _dot(hb, wg_ref[0, :, c0:c0 + fsub])
            up = _dot(hb, wu_ref[0, :, c0:c0 + fsub])
            act = (gt * jax.nn.sigmoid(gt) * up).astype(BF16)
            acc_s[...] += _dot(act, wd_ref[0, c0:c0 + fsub, :])

        @pl.when(f == nf - 1)
        def _():
            pieces = x_ref[:, d:].astype(F32)
            lane = lax.broadcasted_iota(jnp.int32, pieces.shape, 1)
            mine = (lane < 3 * N_EXPERTS) & (lane % N_EXPERTS == te_ref[t])
            gate = jnp.sum(jnp.where(mine, pieces, 0.0), axis=-1, keepdims=True)
            o_ref[...] = (acc_s[...] * gate).astype(BF16)

    @pl.when((t >= nv_ref[0]) & (f == nf - 1))
    def _():
        o_ref[...] = jnp.zeros_like(o_ref)


def _grouped_ffn(xs, tab, wg, wu, wd, tm, tf):
    rows, w = xs.shape
    d = wg.shape[1]
    n_tiles = rows // tm
    nf = wg.shape[2] // tf
    live = lambda t, nv: jnp.minimum(t, nv[0] - 1)
    grid_spec = pltpu.PrefetchScalarGridSpec(
        num_scalar_prefetch=2,
        grid=(n_tiles, nf),
        in_specs=[pl.BlockSpec((tm, w), lambda t, f, te, nv: (live(t, nv), 0)),
                  pl.BlockSpec((1, d, tf), lambda t, f, te, nv: (te[t], 0, f)),
                  pl.BlockSpec((1, d, tf), lambda t, f, te, nv: (te[t], 0, f)),
                  pl.BlockSpec((1, tf, d), lambda t, f, te, nv: (te[t], f, 0))],
        out_specs=pl.BlockSpec((tm, d), lambda t, f, te, nv: (t, 0)),
        scratch_shapes=[pltpu.VMEM((tm, d), F32)])
    return pl.pallas_call(
        _gffn_kernel,
        out_shape=jax.ShapeDtypeStruct((rows, d), BF16),
        grid_spec=grid_spec,
        compiler_params=_params(("arbitrary", "arbitrary")),
        name="moe_grouped_ffn",
    )(tab['tile_e'], tab['n_live'], xs, wg, wu, wd)


def _combine_kernel(lo_ref, nc_ref, off_ref, info_ref, ys_hbm, x_ref, gate_ref, gp_ref, o_ref,
                    yall, sem, *, rmax):
    i = pl.program_id(0)
    n = pl.num_programs(0)
    slot = i % 2
    tb = x_ref.shape[1]

    def chunk_copy(blk, sl, e, c):
        k = blk * N_EXPERTS + e
        src = pl.multiple_of(off_ref[k] + c * MOE_SEG, MOE_SEG)
        dst = pl.multiple_of(lo_ref[k] + c * MOE_SEG, MOE_SEG)
        return pltpu.make_async_copy(ys_hbm.at[pl.ds(src, MOE_SEG)],
                                     yall.at[sl, pl.ds(dst, MOE_SEG)], sem.at[sl])

    def fetch(blk, sl):
        yall[sl, 2 * tb:, :] = jnp.zeros((rmax - 2 * tb, yall.shape[2]), BF16)
        _segment_loops(blk, nc_ref, lambda e, c: chunk_copy(blk, sl, e, c).start())

    @pl.when(i == 0)
    def _():
        fetch(0, 0)

    @pl.when(i + 1 < n)
    def _():
        fetch(i + 1, 1 - slot)

    _segment_loops(i, nc_ref, lambda e, c: chunk_copy(i, slot, e, c).wait())
    rsub = min(MOE_CSUB, tb)
    for r0 in range(0, tb, rsub):
        info = info_ref[0, r0:r0 + rsub, :]
        loc1 = _local_rows(info[:, 0:1], info[:, 2:3], i, lo_ref)
        loc2 = _local_rows(info[:, 1:2], info[:, 3:4], i, lo_ref)
        lane = lax.broadcasted_iota(jnp.int32, (rsub, rmax), 1).astype(F32)
        sel = jnp.where((lane == loc1) | (lane == loc2), 1.0, 0.0).astype(BF16)
        o_ref[0, r0:r0 + rsub, :] = (x_ref[0, r0:r0 + rsub, :]
                                     + gate_ref[0] * _rms(_dot(sel, yall[slot]), gp_ref[...]))


def _combine(ys, info, tab, x3, gate, gp, bpb, rmax):
    nblk, tb, d = x3.shape
    grid_spec = pltpu.PrefetchScalarGridSpec(
        num_scalar_prefetch=3,
        grid=(nblk,),
        in_specs=[pl.BlockSpec((1, tb, LANES), lambda i, *_: (i, 0, 0)),
                  pl.BlockSpec(memory_space=pl.ANY),
                  pl.BlockSpec((1, tb, d), lambda i, *_: (i, 0, 0)),
                  pl.BlockSpec((1, 1, d), lambda i, *_: (i // bpb, 0, 0)),
                  pl.BlockSpec((1, d), lambda i, *_: (0, 0))],
        out_specs=pl.BlockSpec((1, tb, d), lambda i, *_: (i, 0, 0)),
        scratch_shapes=[pltpu.VMEM((2, rmax, d), BF16), pltpu.SemaphoreType.DMA((2,))])
    return pl.pallas_call(
        functools.partial(_combine_kernel, rmax=rmax),
        out_shape=jax.ShapeDtypeStruct((nblk, tb, d), F32),
        grid_spec=grid_spec,
        compiler_params=_params(("arbitrary",)),
        name="moe_combine",
    )(tab['loc_off'], tab['nchunk'], tab['off'], info, ys, x3, gate, gp)


def _moe(x, sc, sh, gpre, router, wg, wu, wd, gate, gpost, tm, tf):
    b, l, d = x.shape
    tb = min(MOE_TB, l)
    bpb = l // tb
    nblk = b * bpb
    x3 = x.reshape(nblk, tb, d)
    hx, info, infot, cnt = _router(x3, sc, sh, gpre, router, bpb)
    cnt = cnt[:, 0, :N_EXPERTS].astype(jnp.int32)
    seg_pad = N_EXPERTS * (MOE_SEG - 1)
    rmax = -(-(2 * tb + seg_pad) // 256) * 256
    n_tiles = -(-(2 * b * l + nblk * seg_pad + N_EXPERTS * (tm - 1)) // tm)
    tab = _routing_tables(cnt, tm, n_tiles)
    xs = _dispatch(hx, infot, tab, n_tiles * tm, rmax)
    ys = _grouped_ffn(xs, tab, wg, wu, wd, tm, tf)
    return _combine(ys, info, tab, x3, gate, gpost, bpb, rmax).reshape(b, l, d)


def _rope_tables(s):
    half = DIFF_DK // 2
    n_freq = half // 2
    inv = ROPE_THETA ** (-jnp.arange(n_freq, dtype=F32) / n_freq)
    t = jnp.arange(s, dtype=jnp.int32)
    rows = (t // GRID_W).astype(F32)[:, None]
    cols = (t % GRID_W).astype(F32)[:, None]
    lane = jnp.arange(LANES, dtype=jnp.int32)
    dd = lane % DIFF_DK
    pos = jnp.where((dd < half)[None, :], rows, cols)
    ang = pos * inv[dd % n_freq][None, :]
    first = ((dd % half) < n_freq)[None, :]
    cos, sin = jnp.cos(ang), jnp.sin(ang)
    return cos, jnp.where(first, -sin, 0.0), jnp.where(first, 0.0, sin)


def _dft_tables(n):
    m = 32 if n % 32 == 0 and n > 32 else 1
    j = jnp.arange(n, dtype=jnp.int32)[:, None]

    def tab(k, period):
        ang = ((j * k[None, :]) % period).astype(F32) * (2.0 * math.pi / period)
        return jnp.cos(ang)[:, :, None], jnp.sin(ang)[:, :, None]

    ca, sa = tab(jnp.arange(n // m, dtype=jnp.int32), n // m)
    cb, sb = tab(jnp.arange(m, dtype=jnp.int32), n)
    cb, sb = jnp.swapaxes(cb, 1, 2), jnp.swapaxes(sb, 1, 2)
    sc = n ** -0.5
    return (((ca * cb - sa * sb) * sc).reshape(n, n), ((sa * cb + ca * sb) * sc).reshape(n, n))


def _channel_dft():
    c, s = _dft_tables(FNET_GROUP_CH)
    eye = jnp.eye(FNET_GROUPS, dtype=F32)
    return jnp.concatenate([jnp.kron(eye, c), jnp.kron(eye, s)], axis=1).astype(BF16)


def kernel(x, c, ctx, c_ctx, ada_w, ada_b, norm_mix_pre, norm_mix_post, norm_ffn_pre, norm_ffn_post,
           w_in, w_out, diff_lq1, diff_lk1, diff_lq2, diff_lk2, diff_subln, fnet_w,
           s5_a_re, s5_a_im, s5_log_dt, s5_b_re, s5_b_im, s5_c_re, s5_c_im, s5_d, s5_w_glu,
           ffn_w_gate, ffn_w_up, ffn_w_down, moe_router, moe_w_gate, moe_w_up, moe_w_down):
    b, s, d = x.shape
    lc = ctx.shape[1]
    depth = ada_w.shape[0]

    cc = jnp.zeros((MOD_ROWS, d), F32).at[:b].set(c).at[b].set(c_ctx)
    mod = _modulation(cc, ada_w, ada_b)

    rope_tabs = _rope_tables(s)
    cs64 = _channel_dft()
    dft = {n: tuple(t.astype(BF16) for t in _dft_tables(n)) for n in (s, lc)}

    xc = ctx
    for l in range(depth):
        need_ctx = l < depth - 1
        lam_init = 0.8 - 0.6 * math.exp(-0.3 * l)
        m_lat = [mod[l, :b, i * d:(i + 1) * d].reshape(b, 1, d) for i in range(6)]
        m_ctx = [jnp.broadcast_to(mod[l, b, i * d:(i + 1) * d].reshape(1, 1, d), (b, 1, d))
                 for i in range(6)]
        row = lambda v: v.reshape(1, -1).astype(F32)
        w_in_bf = w_in[l].astype(BF16)
        w_out_bf = w_out[l].astype(BF16)
        fw_bf = fnet_w[l].astype(BF16)
        wglu_bf = s5_w_glu[l].astype(BF16)
        lams = (row(diff_lq1[l]), row(diff_lk1[l]), row(diff_lq2[l]), row(diff_lk2[l]))
        subln = row(diff_subln[l])

        q, k, v, g1, g2, u = _inproj(x, m_lat[1], m_lat[0], row(norm_mix_pre[l]), w_in_bf, cs64,
                                     rope_tabs, min(INPROJ_TL, s))
        qc, kc, vc, g1c, g2c, uc = _inproj(xc, m_ctx[1], m_ctx[0], row(norm_mix_pre[l]), w_in_bf,
                                           cs64, None, lc)
        a_lat = _attention(q, [(kc, vc), (k, v)], lams, subln, lam_init, min(ATTN_TQ, s))
        f_lat = _fnet_dft(*dft[s], g1, g2, fw_bf, b, min(512, s), 2)

        mats = [_s5_matrices(s5_a_re[l, dr], s5_a_im[l, dr], s5_log_dt[l, dr], s5_b_re[l, dr],
                             s5_b_im[l, dr], s5_c_re[l, dr], s5_c_im[l, dr]) for dr in (0, 1)]
        mats = (mats[0][0], mats[1][0], mats[0][1], mats[1][1], mats[0][2], mats[1][2])
        zero_state = jnp.zeros((b, 2 * S5_LANES), F32)
        ycf, ycr, hcf, hcr = _s5_scan(uc, mats, zero_state, zero_state, b, S5_TC)
        yf, yr, _, _ = _s5_scan(u, mats, hcf, hcr, b, S5_TC)

        x = _outproj(a_lat, f_lat, yf, yr, u, row(s5_d[l]), wglu_bf, w_out_bf,
                     x, m_lat[2], row(norm_mix_post[l]), min(512, s))
        if need_ctx:
            a_ctx = _attention(qc, [(kc, vc)], lams, subln, lam_init, lc)
            f_ctx = _fnet_dft(*dft[lc], g1c, g2c, fw_bf, b, lc, 2)
            xc = _outproj(a_ctx, f_ctx, ycf, ycr, uc, row(s5_d[l]), wglu_bf,
                          w_out_bf, xc, m_ctx[2], row(norm_mix_post[l]), lc)

        i = l // 2
        gpre, gpost = row(norm_ffn_pre[l]), row(norm_ffn_post[l])
        xc1 = xc.reshape(1, b * lc, d)
        if l % 2 == 0:
            wg, wu, wd = (ffn_w_gate[i].astype(BF16), ffn_w_up[i].astype(BF16),
                          ffn_w_down[i].astype(BF16))
            tf = FFN_DENSE_TF
            x = _ffn(x, m_lat[4], m_lat[3], gpre, wg, wu, wd, m_lat[5], gpost, min(1024, s), tf)
            if need_ctx:
                xc1 = _ffn(xc1, m_ctx[4][:1], m_ctx[3][:1], gpre, wg, wu, wd, m_ctx[5][:1], gpost,
                           min(1024, b * lc), tf)
        else:
            wg, wu, wd = (moe_w_gate[i].astype(BF16), moe_w_up[i].astype(BF16),
                          moe_w_down[i].astype(BF16))
            r = jnp.zeros((d, LANES), F32).at[:, :N_EXPERTS].set(moe_router[i])
            r_hi = r.astype(BF16)
            router = jnp.stack([r_hi, (r - r_hi.astype(F32)).astype(BF16)])
            x = _moe(x, m_lat[4], m_lat[3], gpre, router, wg, wu, wd, m_lat[5], gpost,
                     MOE_TM, MOE_TF)
            if need_ctx:
                xc1 = _moe(xc1, m_ctx[4][:1], m_ctx[3][:1], gpre, router, wg, wu, wd, m_ctx[5][:1],
                           gpost, MOE_TM, MOE_TF)
        xc = xc1.reshape(b, lc, d)
    return x
```
